```python
import jax, jax.numpy as jnp
from jax import lax
import numpy as np

D_MODEL = 1024
BATCH = 16
SEQ = 256
DEPTH = 4
DEC_BATCH = 8
DEC_SEQ = 4096
PAST_LEN = 256

GRID_W = 64
MIX_WIDTH = D_MODEL
GLA_WIDTH = MIX_WIDTH // 2
GLA_HEADS = 4
GLA_DK = GLA_WIDTH // 2 // GLA_HEADS
GLA_DV = GLA_WIDTH // GLA_HEADS
GLA_QK = GLA_HEADS * GLA_DK
GLA_RANK = 16
GLA_NORMALIZER = 16.0
GLA_CHUNK = 64
RWKV_WIDTH = MIX_WIDTH - GLA_WIDTH
RWKV_N = 64
RWKV_HEADS = RWKV_WIDTH // RWKV_N
DECAY_RANK = 64
AAA_RANK = 64
GATE_RANK = 128
N_EXPERTS = 64
TOP_K = 8
N_GROUPS = 8
TOPK_GROUPS = 4
EXPERT_FF = D_MODEL // 4
ROUTED_SCALE = 2.5
MOE_BLOCK = 256
EPS = 1e-6
RWKV_LN_EPS = 64e-5
GLA_COLS = 2 * GLA_QK + 2 * GLA_WIDTH + 2 * GLA_RANK
RWKV_COLS = 3 * RWKV_WIDTH + 2 * DECAY_RANK + AAA_RANK + GATE_RANK
PROJ_COLS = GLA_COLS + RWKV_COLS

kernel_name = 'hybrid_gla_rwkv7_moe_diffusion_step'


def rmsnorm(x, g):
    xf = x.astype(jnp.float32)
    y = xf * lax.rsqrt(jnp.mean(xf * xf, axis=-1, keepdims=True) + EPS)
    return (y * g).astype(x.dtype)


def adaln_params(cvec, w, b):
    mod = jax.nn.silu(cvec) @ w + b
    return jnp.split(mod[..., None, :], 6, axis=-1)


def flip_t(a):
    return jnp.flip(a, axis=1)


def conv_latent(u, wconv):
    b, t, ch = u.shape
    rows = t // GRID_W
    img = u.reshape(b, rows, GRID_W, ch)
    out = lax.conv_general_dilated(img, wconv[:, :, None, :], window_strides=(1, 1), padding='SAME',
                                   dimension_numbers=('NHWC', 'HWIO', 'NHWC'), feature_group_count=ch)
    return out.reshape(b, t, ch)


def conv_context(u, wconv):
    ch = u.shape[-1]
    return lax.conv_general_dilated(u, wconv[1][:, None, :], window_strides=(1,), padding='SAME',
                                    dimension_numbers=('NWC', 'WIO', 'NWC'), feature_group_count=ch)


def gla_chunked(q, k, v, gk, s0):
    b, t, h, dk = q.shape
    dv = v.shape[-1]
    nc = t // GLA_CHUNK
    f32 = jnp.float32

    def to_chunks(a):
        return a.astype(f32).reshape(b, nc, GLA_CHUNK, h, a.shape[-1]).transpose(1, 0, 3, 2, 4)

    causal = jnp.tril(jnp.ones((GLA_CHUNK, GLA_CHUNK), dtype=bool))

    def step(s, xs):
        qc, kc, vc, gc = xs
        cum = jnp.cumsum(gc, axis=-2)
        o_inter = jnp.einsum('bhcd,bhde->bhce', qc * jnp.exp(cum), s)
        rel = jnp.where(causal[:, :, None], cum[..., :, None, :] - cum[..., None, :, :], -jnp.inf)
        attn = jnp.einsum('bhid,bhjd,bhijd->bhij', qc, kc, jnp.exp(rel))
        o = o_inter + jnp.einsum('bhij,bhje->bhie', attn, vc)
        s = jnp.exp(cum[..., -1, :])[..., None] * s + jnp.einsum('bhcd,bhce->bhde', kc * jnp.exp(cum[..., -1:, :] - cum), vc)
        return s, o

    s, o = lax.scan(step, s0.astype(f32), (to_chunks(q), to_chunks(k), to_chunks(v), to_chunks(gk)))
    return o.transpose(1, 0, 3, 2, 4).reshape(b, t, h, dv), s


def gla_mixer(u, s_fwd, s_bwd, w_dec, b_dec, norm_g):
    b, t, _ = u.shape
    f32 = jnp.float32
    q, k, v, g, z_f, z_b = jnp.split(u, np.cumsum([GLA_QK, GLA_QK, GLA_WIDTH, GLA_WIDTH, GLA_RANK]).tolist(), axis=-1)
    q = q.reshape(b, t, GLA_HEADS, GLA_DK) * (GLA_DK ** -0.5)
    k = k.reshape(b, t, GLA_HEADS, GLA_DK)
    v = v.reshape(b, t, GLA_HEADS, GLA_DV)

    def log_decay(z, w, bias):
        return (jax.nn.log_sigmoid((z @ w + bias).astype(f32)) / GLA_NORMALIZER).reshape(b, t, GLA_HEADS, GLA_DK)

    o_f, s_f = gla_chunked(q, k, v, log_decay(z_f, w_dec[0], b_dec[0]), s_fwd)
    o_b, s_b = gla_chunked(flip_t(q), flip_t(k), flip_t(v), flip_t(log_decay(z_b, w_dec[1], b_dec[1])), s_bwd)
    o = o_f + flip_t(o_b)
    o = o * lax.rsqrt(jnp.mean(o * o, axis=-1, keepdims=True) + EPS)
    o = o.reshape(b, t, GLA_WIDTH) * norm_g * jax.nn.silu(g.astype(f32))
    return o.astype(u.dtype), s_f, s_b


def rwkv_scan(r, w, k, v, kk, a, s0):
    def step(s, xs):
        rt, wt, kt, vt, kkt, at = xs
        sa = jnp.einsum('bhvk,bhk->bhv', s, -kkt)
        s = s * wt[:, :, None, :] + sa[..., None] * (kkt * at)[:, :, None, :] + vt[..., None] * kt[:, :, None, :]
        return s, jnp.einsum('bhvk,bhk->bhv', s, rt)

    xs = tuple(jnp.moveaxis(z, 1, 0) for z in (r, w, k, v, kk, a))
    s, y = lax.scan(step, s0.astype(jnp.float32), xs)
    return jnp.moveaxis(y, 0, 1), s


def rwkv_mixer(u, s_fwd, s_bwd, w2, w0, a2, a0, g2, k_k, k_a, r_k, ln_g, ln_b):
    b, t, _ = u.shape
    f32 = jnp.float32
    uf = u.astype(f32)
    r, k, v, z_wf, z_wb, z_a, z_g = jnp.split(uf, np.cumsum([RWKV_WIDTH] * 3 + [DECAY_RANK] * 2 + [AAA_RANK]).tolist(), axis=-1)

    def hd(z):
        return z.reshape(b, t, RWKV_HEADS, RWKV_N)

    def ph(p):
        return p.astype(f32).reshape(RWKV_HEADS, RWKV_N)

    def decay(z, w2d, w0d):
        w = -jax.nn.softplus(-(w0d.astype(f32) + jnp.tanh(z) @ w2d.astype(f32))) - 0.5
        return hd(jnp.exp(-jnp.exp(w)))

    a = hd(jax.nn.sigmoid(a0.astype(f32) + z_a @ a2.astype(f32)))
    g = jax.nn.sigmoid(z_g) @ g2.astype(f32)
    r, k, v = hd(r), hd(k), hd(v)
    kk = k * ph(k_k)
    kk = kk * lax.rsqrt(jnp.maximum(jnp.sum(kk * kk, axis=-1, keepdims=True), 1e-24))
    k = k * (1.0 + (a - 1.0) * ph(k_a))
    y_f, s_f = rwkv_scan(r, decay(z_wf, w2[0], w0[0]), k, v, kk, a, s_fwd)
    y_b, s_b = rwkv_scan(flip_t(r), flip_t(decay(z_wb, w2[1], w0[1])), flip_t(k), flip_t(v), flip_t(kk), flip_t(a), s_bwd)
    y = y_f + flip_t(y_b)
    mu = jnp.mean(y, axis=-1, keepdims=True)
    var = jnp.mean(jnp.square(y - mu), axis=-1, keepdims=True)
    y = (y - mu) * lax.rsqrt(var + RWKV_LN_EPS) * ph(ln_g) + ph(ln_b)
    bonus = jnp.sum(r * k * ph(r_k), axis=-1, keepdims=True) * v
    out = (y + bonus).reshape(b, t, RWKV_WIDTH) * g
    return out.astype(u.dtype), s_f, s_b


def swiglu(x, w1, w3, w2):
    return (jax.nn.silu(x @ w1) * (x @ w3)) @ w2


def route(xt, w_router, bias):
    t = xt.shape[0]
    f32 = jnp.float32
    scores = jax.nn.sigmoid(xt.astype(f32) @ w_router.astype(f32))
    sel = scores + bias.astype(f32)
    grp_top2, _ = lax.top_k(sel.reshape(t, N_GROUPS, N_EXPERTS // N_GROUPS), 2)
    _, gidx = lax.top_k(jnp.sum(grp_top2, axis=-1), TOPK_GROUPS)
    gmask = jnp.sum(jax.nn.one_hot(gidx, N_GROUPS, dtype=jnp.int32), axis=-2) > 0
    sel = jnp.where(jnp.repeat(gmask, N_EXPERTS // N_GROUPS, axis=-1), sel, -jnp.inf)
    _, idx = lax.top_k(sel, TOP_K)
    wts = jnp.take_along_axis(scores, idx, axis=-1)
    wts = wts / jnp.sum(wts, axis=-1, keepdims=True) * ROUTED_SCALE
    return idx, wts


def routed_experts(xt, idx, wts, w1, w3, w2):
    t, d = xt.shape
    n_assign = t * TOP_K
    n_blocks = -(-n_assign // MOE_BLOCK) + N_EXPERTS
    e_flat = idx.reshape(-1)
    tok_flat = jnp.arange(n_assign, dtype=jnp.int32) // TOP_K
    order = jnp.argsort(e_flat)
    e_sorted = e_flat[order]
    counts = jnp.bincount(e_flat, length=N_EXPERTS)
    padded = (counts + MOE_BLOCK - 1) // MOE_BLOCK * MOE_BLOCK
    pad_end = jnp.cumsum(padded)
    pad_start = pad_end - padded
    start = jnp.cumsum(counts) - counts
    dest = pad_start[e_sorted] + jnp.arange(n_assign, dtype=jnp.int32) - start[e_sorted]
    slot_tok = jnp.full((n_blocks * MOE_BLOCK,), t, jnp.int32).at[dest].set(tok_flat[order])
    slot_w = jnp.zeros((n_blocks * MOE_BLOCK,), xt.dtype).at[dest].set(wts.reshape(-1)[order].astype(xt.dtype))
    blk_e = jnp.minimum(jnp.searchsorted(pad_end, jnp.arange(n_blocks, dtype=jnp.int32) * MOE_BLOCK, side='right'), N_EXPERTS - 1)
    x_pad = jnp.concatenate([xt, jnp.zeros((1, d), xt.dtype)], axis=0)

    def step(y, blk):
        ids, wb, e = blk
        return y.at[ids].add(swiglu(x_pad[ids], w1[e], w3[e], w2[e]) * wb[:, None]), None

    y, _ = lax.scan(step, jnp.zeros_like(x_pad),
                    (slot_tok.reshape(n_blocks, MOE_BLOCK), slot_w.reshape(n_blocks, MOE_BLOCK), blk_e))
    return y[:t]


def moe(h, w_router, bias, w1, w3, w2, ws1, ws3, ws2):
    xt = h.reshape(-1, h.shape[-1])
    idx, wts = route(xt, w_router, bias)
    y = routed_experts(xt, idx, wts, w1, w3, w2) + swiglu(xt, ws1, ws3, ws2)
    return y.reshape(h.shape)


def layer_forward(x, mod, latent, s_gla, s_rwkv, p):
    sh1, sc1, g1, sh2, sc2, g2 = mod
    h = rmsnorm(x, p['norm1']) * (1.0 + sc1) + sh1
    u = h @ p['w_in']
    u_gla, u_rwkv = u[..., :GLA_COLS], u[..., GLA_COLS:]
    u_rwkv = conv_latent(u_rwkv, p['rwkv_conv']) if latent else conv_context(u_rwkv, p['rwkv_conv'])
    o_gla, sgf, sgb = gla_mixer(u_gla, s_gla[0], s_gla[1], p['gla_w_dec'], p['gla_b_dec'], p['gla_norm'])
    o_rwkv, srf, srb = rwkv_mixer(u_rwkv, s_rwkv[0], s_rwkv[1], p['rwkv_w2'], p['rwkv_w0'], p['rwkv_a2'], p['rwkv_a0'],
                                  p['rwkv_g2'], p['rwkv_k_k'], p['rwkv_k_a'], p['rwkv_r_k'], p['rwkv_ln_g'], p['rwkv_ln_b'])
    x = x + g1 * (jnp.concatenate([o_gla, o_rwkv], axis=-1) @ p['w_out'])
    h = rmsnorm(x, p['norm2']) * (1.0 + sc2) + sh2
    x = x + g2 * moe(h, p['router_w'], p['router_b'], p['exp_w1'], p['exp_w3'], p['exp_w2'], p['sh_w1'], p['sh_w3'], p['sh_w2'])
    return x, (sgf, sgb), (srf, srb)


def setup_inputs(seed: int = 0) -> dict:
    key = jax.random.key(seed)
    ks = iter(jax.random.split(key, 48))

    def nrm(shape, s):
        return s * jax.random.normal(next(ks), shape, jnp.float32)

    D, E, F = D_MODEL, N_EXPERTS, EXPERT_FF
    conv = nrm((DEPTH, 3, 3, RWKV_COLS), 0.2).at[:, 1, 1].add(1.0)
    return {
        'x_prompt': nrm((BATCH, SEQ, D), 1.0),
        'x_sample': nrm((DEC_BATCH, DEC_SEQ, D), 1.0),
        'c': nrm((DEC_BATCH, D), 1.0),
        'state_gla': nrm((DEC_BATCH, DEPTH, 2, GLA_HEADS, GLA_DK, GLA_DV), 0.5),
        'state_rwkv': nrm((DEC_BATCH, DEPTH, 2, RWKV_HEADS, RWKV_N, RWKV_N), 0.3),
        'c_ctx': nrm((D,), 1.0),
        'ada_w': nrm((DEPTH, D, 6 * D), 0.3 * D ** -0.5),
        'ada_b': nrm((DEPTH, 6 * D), 0.02),
        'norm1': 1.0 + nrm((DEPTH, D), 0.05),
        'norm2': 1.0 + nrm((DEPTH, D), 0.05),
        'norm_f': 1.0 + nrm((D,), 0.05),
        'w_in': nrm((DEPTH, D, PROJ_COLS), D ** -0.5),
        'w_out': nrm((DEPTH, MIX_WIDTH, D), MIX_WIDTH ** -0.5),
        'gla_w_dec': nrm((DEPTH, 2, GLA_RANK, GLA_QK), GLA_RANK ** -0.5),
        'gla_b_dec': nrm((DEPTH, 2, GLA_QK), 0.1),
        'gla_norm': 1.0 + nrm((DEPTH, GLA_WIDTH), 0.05),
        'rwkv_conv': conv,
        'rwkv_w2': nrm((DEPTH, 2, DECAY_RANK, RWKV_WIDTH), 0.5 * DECAY_RANK ** -0.5),
        'rwkv_w0': -1.0 + nrm((DEPTH, 2, RWKV_WIDTH), 0.5),
        'rwkv_a2': nrm((DEPTH, AAA_RANK, RWKV_WIDTH), AAA_RANK ** -0.5),
        'rwkv_a0': nrm((DEPTH, RWKV_WIDTH), 0.1),
        'rwkv_g2': nrm((DEPTH, GATE_RANK, RWKV_WIDTH), GATE_RANK ** -0.5),
        'rwkv_k_k': 0.85 + nrm((DEPTH, RWKV_WIDTH), 0.05),
        'rwkv_k_a': 1.0 + nrm((DEPTH, RWKV_WIDTH), 0.05),
        'rwkv_r_k': nrm((DEPTH, RWKV_WIDTH), 0.1),
        'rwkv_ln_g': 1.0 + nrm((DEPTH, RWKV_WIDTH), 0.05),
        'rwkv_ln_b': nrm((DEPTH, RWKV_WIDTH), 0.02),
        'router_w': nrm((DEPTH, D, E), D ** -0.5),
        'router_b': nrm((DEPTH, E), 0.01),
        'exp_w1': nrm((DEPTH, E, D, F), D ** -0.5),
        'exp_w3': nrm((DEPTH, E, D, F), D ** -0.5),
        'exp_w2': nrm((DEPTH, E, F, D), F ** -0.5),
        'sh_w1': nrm((DEPTH, D, F), D ** -0.5),
        'sh_w3': nrm((DEPTH, D, F), D ** -0.5),
        'sh_w2': nrm((DEPTH, F, D), F ** -0.5),
    }


def reference(x_prompt, x_sample, c, state_gla, state_rwkv, c_ctx, ada_w, ada_b, norm1, norm2, norm_f, w_in, w_out,
              gla_w_dec, gla_b_dec, gla_norm, rwkv_conv, rwkv_w2, rwkv_w0, rwkv_a2, rwkv_a0, rwkv_g2, rwkv_k_k, rwkv_k_a,
              rwkv_r_k, rwkv_ln_g, rwkv_ln_b, router_w, router_b, exp_w1, exp_w3, exp_w2, sh_w1, sh_w3, sh_w2):
    bp = x_prompt.shape[0]
    zero_gla = jnp.zeros((bp, GLA_HEADS, GLA_DK, GLA_DV), jnp.float32)
    zero_rwkv = jnp.zeros((bp, RWKV_HEADS, RWKV_N, RWKV_N), jnp.float32)
    xp, xs = x_prompt, x_sample
    new_gla, new_rwkv = [], []
    for l in range(DEPTH):
        p = {'norm1': norm1[l], 'norm2': norm2[l], 'w_in': w_in[l], 'w_out': w_out[l],
             'gla_w_dec': gla_w_dec[l], 'gla_b_dec': gla_b_dec[l], 'gla_norm': gla_norm[l],
             'rwkv_conv': rwkv_conv[l], 'rwkv_w2': rwkv_w2[l], 'rwkv_w0': rwkv_w0[l], 'rwkv_a2': rwkv_a2[l],
             'rwkv_a0': rwkv_a0[l], 'rwkv_g2': rwkv_g2[l], 'rwkv_k_k': rwkv_k_k[l], 'rwkv_k_a': rwkv_k_a[l],
             'rwkv_r_k': rwkv_r_k[l], 'rwkv_ln_g': rwkv_ln_g[l], 'rwkv_ln_b': rwkv_ln_b[l],
             'router_w': router_w[l], 'router_b': router_b[l], 'exp_w1': exp_w1[l], 'exp_w3': exp_w3[l],
             'exp_w2': exp_w2[l], 'sh_w1': sh_w1[l], 'sh_w3': sh_w3[l], 'sh_w2': sh_w2[l]}
        xp, s_g, s_r = layer_forward(xp, adaln_params(c_ctx, ada_w[l], ada_b[l]), False,
                                     (zero_gla, zero_gla), (zero_rwkv, zero_rwkv), p)
        new_gla.append(jnp.stack(s_g, axis=1))
        new_rwkv.append(jnp.stack(s_r, axis=1))
        xs, _, _ = layer_forward(xs, adaln_params(c, ada_w[l], ada_b[l]), True,
                                 (state_gla[:, l, 0], state_gla[:, l, 1]),
                                 (state_rwkv[:, l, 0], state_rwkv[:, l, 1]), p)
    y_prompt = rmsnorm(xp, norm_f)
    y_sample = rmsnorm(xs, norm_f)
    new_state_gla = jnp.stack(new_gla, axis=1)
    new_state_rwkv = jnp.stack(new_rwkv, axis=1)
    return (y_prompt, y_sample, new_state_gla, new_state_rwkv)
```

```python
import functools

import numpy as np
import jax
import jax.numpy as jnp
from jax import lax
from jax.experimental import pallas as pl
from jax.experimental.pallas import tpu as pltpu

F32 = jnp.float32
BF16 = jnp.bfloat16
HI = lax.Precision.HIGHEST

LANES = 128
EPS = 1e-6
RWKV_LN_EPS = 64e-5
GLA_NORMALIZER = 16.0
ROUTED_SCALE = 2.5
N_GROUPS = 8
TOPK_GROUPS = 4
TOP_K = 8
CHUNK = 64
VMEM_LIMIT = 56 * 1024 * 1024


def _cparams(sem):
    return pltpu.CompilerParams(dimension_semantics=sem, vmem_limit_bytes=VMEM_LIMIT)


def _dot(a, b, dims=(((1,), (0,)), ((), ())), precision=None):
    return lax.dot_general(a, b, dims, precision=precision, preferred_element_type=F32)


def _bdot(a, b, dims=(((1,), (0,)), ((), ()))):
    return lax.dot_general(a.astype(BF16), b.astype(BF16), dims, preferred_element_type=F32)


NT = (((1,), (1,)), ((), ()))
TN = (((0,), (0,)), ((), ()))


def _sigmoid(x):
    return 1.0 / (1.0 + jnp.exp(-x))


def _silu(x):
    return x * _sigmoid(x)


def _softplus(x):
    return jnp.maximum(x, 0.0) + jnp.log(1.0 + jnp.exp(-jnp.abs(x)))


def _log_sigmoid(x):
    return -_softplus(-x)


def _rms_mod(x, g, scale, shift):
    y = x * lax.rsqrt(jnp.mean(x * x, axis=-1, keepdims=True) + EPS)
    return (y * g) * (1.0 + scale) + shift


def _adaln_kernel(c_ref, w_ref, b_ref, o_ref):
    c = c_ref[...]
    o_ref[0] = _dot(_silu(c), w_ref[0], precision=HI) + b_ref[0]


def adaln(cvec, ada_w, ada_b, tn=512):
    depth, d, n = ada_w.shape
    r = cvec.shape[0]
    return pl.pallas_call(
        _adaln_kernel,
        grid=(depth, n // tn),
        in_specs=[pl.BlockSpec((r, d), lambda l, j: (0, 0)),
                  pl.BlockSpec((1, d, tn), lambda l, j: (l, 0, j)),
                  pl.BlockSpec((1, 1, tn), lambda l, j: (l, 0, j))],
        out_specs=pl.BlockSpec((1, r, tn), lambda l, j: (l, 0, j)),
        out_shape=jax.ShapeDtypeStruct((depth, r, n), F32),
        compiler_params=_cparams(("parallel", "parallel")),
    )(cvec, ada_w, ada_b.reshape(depth, 1, n))


def _proj_in_kernel(x_ref, mod_ref, g_ref, w_ref, og_ref, or_ref, *, n_gla):
    m = mod_ref[0]
    h = _rms_mod(x_ref[0], g_ref[...], m[1:2], m[0:1])
    u = _bdot(h, w_ref[...])
    og_ref[0] = u[:, :n_gla]
    or_ref[0] = u[:, n_gla:]


def proj_in(x, mod, mod_row, norm_g, w, n_gla, tm=256):
    b, t, d = x.shape
    n = w.shape[1]
    tm = min(tm, t)
    return pl.pallas_call(
        functools.partial(_proj_in_kernel, n_gla=n_gla),
        grid=(b, t // tm),
        in_specs=[pl.BlockSpec((1, tm, d), lambda i, j: (i, j, 0)),
                  pl.BlockSpec((1, 6, d), lambda i, j: (mod_row(i), 0, 0)),
                  pl.BlockSpec((1, d), lambda i, j: (0, 0)),
                  pl.BlockSpec((d, n), lambda i, j: (0, 0))],
        out_specs=[pl.BlockSpec((1, tm, n_gla), lambda i, j: (i, j, 0)),
                   pl.BlockSpec((1, tm, n - n_gla), lambda i, j: (i, j, 0))],
        out_shape=[jax.ShapeDtypeStruct((b, t, n_gla), F32),
                   jax.ShapeDtypeStruct((b, t, n - n_gla), F32)],
        compiler_params=_cparams(("parallel", "parallel")),
    )(x, mod, norm_g.reshape(1, d), w)


CONV_PAD = 8


def _conv_kernel(u_ref, w_ref, o_ref, pad_ref, *, hh, ww):
    pad_ref[...] = jnp.zeros(pad_ref.shape, F32)
    pad_ref[1:hh + 1, CONV_PAD:CONV_PAD + ww, :] = u_ref[0]
    w = w_ref[...]
    acc = jnp.zeros((hh, ww, u_ref.shape[-1]), F32)
    for dy in range(3):
        if hh == 1 and dy != 1:
            continue
        for dx in range(3):
            acc = acc + pad_ref[dy:dy + hh, CONV_PAD - 1 + dx:CONV_PAD - 1 + dx + ww, :] * w[dy, dx]
    o_ref[0] = acc


def conv(u, wconv, hh, ww):
    b, t, c = u.shape
    out = pl.pallas_call(
        functools.partial(_conv_kernel, hh=hh, ww=ww),
        grid=(b, c // LANES),
        in_specs=[pl.BlockSpec((1, hh, ww, LANES), lambda i, j: (i, 0, 0, j)),
                  pl.BlockSpec((3, 3, LANES), lambda i, j: (0, 0, j))],
        out_specs=pl.BlockSpec((1, hh, ww, LANES), lambda i, j: (i, 0, 0, j)),
        out_shape=jax.ShapeDtypeStruct((b, hh, ww, c), F32),
        scratch_shapes=[pltpu.VMEM((hh + 2, ww + 2 * CONV_PAD, LANES), F32)],
        compiler_params=_cparams(("parallel", "parallel")),
    )(u.reshape(b, hh, ww, c), wconv)
    return out.reshape(b, t, c)


GLA_LEVELS = (32, 16, 8, 4, 2, 1)


def _order_consts(rev):
    c = CHUNK
    idx = np.arange(c)
    pos = (c - 1 - idx) if rev else idx
    incl = (pos[None, :] <= pos[:, None])
    strict = (pos[None, :] < pos[:, None])
    return pos, incl, strict


def _gla_consts():
    c = CHUNK
    mats, masks = [], []
    for rev in (False, True):
        pos, incl, strict = _order_consts(rev)
        later = (pos[None, :] > pos[:, None])
        rows = [incl, later]
        pins, sxs, lm = [], [], [np.eye(c, dtype=bool)]
        for s in GLA_LEVELS:
            blk = pos // s
            same = blk[None, :] == blk[:, None]
            pins.append(same & incl)
            sxs.append(same & later)
            lm.append((blk[:, None] % 2 == 1) & (blk[None, :] == blk[:, None] - 1))
        mats.append(np.concatenate(rows + pins + sxs, axis=0).astype(np.float32))
        masks.append(np.stack(lm).astype(np.float32))
    return np.stack(mats), np.stack(masks)


def _rwkv_consts():
    c = CHUNK
    out = []
    for rev in (False, True):
        pos, incl, strict = _order_consts(rev)
        b16 = pos // 16
        b32 = pos // 32
        same16 = b16[None, :] == b16[:, None]
        same32 = b32[None, :] == b32[:, None]
        out.append(np.stack([incl, strict, strict & same16, strict & same32 & ~same16,
                             strict & ~same32, np.eye(c, dtype=bool)]).astype(np.float32))
    return np.stack(out)


def _gla_kernel(u_ref, s0_ref, cst_ref, msk_ref, wd_ref, bd_ref, o_ref, sf_ref, s_scr, *, heads, dk, dv):
    ci = pl.program_id(2)

    @pl.when(ci == 0)
    def _():
        s_scr[...] = s0_ref[0, 0]

    c = CHUNK
    qk = heads * dk
    u = u_ref[0]
    q = u[:, 0:qk] * (dk ** -0.5)
    k = u[:, qk:2 * qk]
    v = u[:, 2 * qk:2 * qk + heads * dv]
    z = u[:, 2 * qk + 2 * heads * dv:]
    gk = _log_sigmoid(_dot(z, wd_ref[0], precision=HI) + bd_ref[0]) * (1.0 / GLA_NORMALIZER)
    big = _dot(cst_ref[0], gk, precision=HI)
    e = jnp.exp(big)
    e_cum, e_rest = e[0:c], e[c:2 * c]
    nl = len(GLA_LEVELS)
    msk = msk_ref[0]
    etot = jnp.exp(jnp.sum(gk, axis=0, keepdims=True))
    outs = []
    for h in range(heads):
        sl = slice(h * dk, (h + 1) * dk)
        qh, kh = q[:, sl], k[:, sl]
        vh = v[:, h * dv:(h + 1) * dv]
        attn = jnp.where(msk[0] > 0, _bdot(qh, kh, NT), 0.0)
        for li in range(nl):
            qs = qh * e[(2 + li) * c:(3 + li) * c, sl]
            ks = kh * e[(2 + nl + li) * c:(3 + nl + li) * c, sl]
            attn = attn + jnp.where(msk[1 + li] > 0, _bdot(qs, ks, NT), 0.0)
        st = s_scr[h]
        o = _bdot(qh * e_cum[:, sl], st, NT) + _bdot(attn, vh)
        outs.append(o)
        s_scr[h] = st * etot[:, sl] + _bdot(vh, kh * e_rest[:, sl], TN)
    o_ref[0, 0] = jnp.concatenate(outs, axis=-1)

    @pl.when(ci == pl.num_programs(2) - 1)
    def _():
        sf_ref[0, 0] = s_scr[...]


def gla(u_gla, s0t, w_dec, b_dec, heads, dk, dv):
    b, t, cols = u_gla.shape
    nc = t // CHUNK
    qk = heads * dk
    rank = w_dec.shape[1]
    cst, msk = _gla_consts()
    wd = jnp.zeros((2, LANES, qk), F32)
    wd = wd.at[0, 0:rank].set(w_dec[0]).at[1, rank:2 * rank].set(w_dec[1])

    def tok(i, d, j):
        return (i, j + d * (nc - 1 - 2 * j), 0)

    return pl.pallas_call(
        functools.partial(_gla_kernel, heads=heads, dk=dk, dv=dv),
        grid=(b, 2, nc),
        in_specs=[pl.BlockSpec((1, CHUNK, cols), tok),
                  pl.BlockSpec((1, 1, heads, dv, dk), lambda i, d, j: (i, d, 0, 0, 0)),
                  pl.BlockSpec((1,) + cst.shape[1:], lambda i, d, j: (d, 0, 0)),
                  pl.BlockSpec((1,) + msk.shape[1:], lambda i, d, j: (d, 0, 0, 0)),
                  pl.BlockSpec((1, LANES, qk), lambda i, d, j: (d, 0, 0)),
                  pl.BlockSpec((1, 1, qk), lambda i, d, j: (d, 0, 0))],
        out_specs=[pl.BlockSpec((1, 1, CHUNK, heads * dv), lambda i, d, j: (d,) + tok(i, d, j)),
                   pl.BlockSpec((1, 1, heads, dv, dk), lambda i, d, j: (i, d, 0, 0, 0))],
        out_shape=[jax.ShapeDtypeStruct((2, b, t, heads * dv), F32),
                   jax.ShapeDtypeStruct((b, 2, heads, dv, dk), F32)],
        scratch_shapes=[pltpu.VMEM((heads, dv, dk), F32)],
        compiler_params=_cparams(("parallel", "parallel", "arbitrary")),
    )(u_gla, s0t, jnp.asarray(cst), jnp.asarray(msk), wd, b_dec.reshape(2, 1, qk))


def _tri_inverse(l, msk):
    eye = msk[5]
    ld = l * msk[2]
    x = eye - ld
    p = _dot(ld, ld, precision=HI)
    x = x + _dot(x, p, precision=HI)
    p = _dot(p, p, precision=HI)
    x = x + _dot(x, p, precision=HI)
    p = _dot(p, p, precision=HI)
    x = x + _dot(x, p, precision=HI)
    for mi in (3, 4):
        x = x - _dot(x, _dot(l * msk[mi], x, precision=HI), precision=HI)
    return x


def _rwkv_kernel(uf_ref, ub_ref, s0_ref, msk_ref, w2_ref, w0_ref, a2_ref, a0_ref, g2_ref, kk_ref, ka_ref, rk_ref,
                 bones_ref, yf_ref, yb_ref, bonus_ref, gate_ref, sf_ref, s_scr, *, heads, n):
    ci = pl.program_id(1)

    @pl.when(ci == 0)
    def _():
        s_scr[...] = s0_ref[0]

    c = CHUNK
    wd = heads * n
    bones = bones_ref[...]

    def seg_sum(x):
        return _dot(x, bones, precision=HI)

    for d, u_ref in ((0, uf_ref), (1, ub_ref)):
        u = u_ref[0]
        msk = msk_ref[d]
        r, k, v = u[:, 0:wd], u[:, wd:2 * wd], u[:, 2 * wd:3 * wd]
        z0 = 3 * wd
        zw = u[:, z0 + d * n:z0 + (d + 1) * n]
        za = u[:, z0 + 2 * n:z0 + 3 * n]
        wlog = -_softplus(-(w0_ref[d] + _dot(jnp.tanh(zw), w2_ref[d], precision=HI))) - 0.5
        logw = -jnp.exp(wlog)
        a = _sigmoid(a0_ref[...] + _dot(za, a2_ref[...], precision=HI))
        kk = k * kk_ref[...]
        kk = kk * lax.rsqrt(jnp.maximum(seg_sum(kk * kk), 1e-24))
        km = k * (1.0 + (a - 1.0) * ka_ref[...])
        bb = kk * a
        cum = _dot(msk[0], logw, precision=HI)
        tot = jnp.sum(logw, axis=0, keepdims=True)
        e_cum = jnp.exp(cum)
        e_neg = jnp.exp(-cum)
        e_rest = jnp.exp(tot - cum)
        e_tot = jnp.exp(tot)
        rt = r * e_cum
        at = kk * jnp.exp(cum - logw)
        kt = km * e_neg
        bt = bb * e_neg
        kh_ = km * e_rest
        bh_ = bb * e_rest
        strict = msk[1] > 0
        incl = msk[0] > 0
        ys = []
        for h in range(heads):
            sl = slice(h * n, (h + 1) * n)
            st = s_scr[d, h]
            ah, rh, vh = at[:, sl], rt[:, sl], v[:, sl]
            lab = jnp.where(strict, _bdot(ah, bt[:, sl], NT), 0.0)
            lak = jnp.where(strict, _bdot(ah, kt[:, sl], NT), 0.0)
            mrk = jnp.where(incl, _bdot(rh, kt[:, sl], NT), 0.0)
            mrb = jnp.where(incl, _bdot(rh, bt[:, sl], NT), 0.0)
            rhs = _bdot(ah, st, NT) + _bdot(lak, vh)
            uu = _dot(_tri_inverse(lab, msk), rhs, precision=HI)
            ys.append(_bdot(rh, st, NT) + _bdot(mrk, vh) - _bdot(mrb, uu))
            s_scr[d, h] = st * e_tot[:, sl] + _bdot(vh, kh_[:, sl], TN) - _bdot(uu, bh_[:, sl], TN)
        (yf_ref, yb_ref)[d][0] = jnp.concatenate(ys, axis=-1)
        if d == 0:
            bonus_ref[0] = seg_sum(r * km * rk_ref[...]) * v
            zg = u[:, z0 + 3 * n:z0 + 3 * n + g2_ref.shape[0]]
            gate_ref[0] = _dot(_sigmoid(zg), g2_ref[...], precision=HI)

    @pl.when(ci == pl.num_programs(1) - 1)
    def _():
        sf_ref[0] = s_scr[...]


def rwkv(u, s0, w2, w0, a2, a0, g2, k_k, k_a, r_k, heads, n):
    b, t, cols = u.shape
    nc = t // CHUNK
    wd = heads * n
    msk = _rwkv_consts()
    bones = np.kron(np.eye(heads, dtype=np.float32), np.ones((n, n), np.float32))
    row = lambda x: x.reshape(1, wd)
    full = lambda a: pl.BlockSpec(a.shape, lambda i, j: (0,) * a.ndim)
    args = [jnp.asarray(msk), w2, w0.reshape(2, 1, wd), a2, row(a0), g2, row(k_k), row(k_a), row(r_k),
            jnp.asarray(bones)]
    return pl.pallas_call(
        functools.partial(_rwkv_kernel, heads=heads, n=n),
        grid=(b, nc),
        in_specs=[pl.BlockSpec((1, CHUNK, cols), lambda i, j: (i, j, 0)),
                  pl.BlockSpec((1, CHUNK, cols), lambda i, j: (i, nc - 1 - j, 0)),
                  pl.BlockSpec((1, 2, heads, n, n), lambda i, j: (i, 0, 0, 0, 0))] + [full(a) for a in args],
        out_specs=[pl.BlockSpec((1, CHUNK, wd), lambda i, j: (i, j, 0)),
                   pl.BlockSpec((1, CHUNK, wd), lambda i, j: (i, nc - 1 - j, 0)),
                   pl.BlockSpec((1, CHUNK, wd), lambda i, j: (i, j, 0)),
                   pl.BlockSpec((1, CHUNK, wd), lambda i, j: (i, j, 0)),
                   pl.BlockSpec((1, 2, heads, n, n), lambda i, j: (i, 0, 0, 0, 0))],
        out_shape=[jax.ShapeDtypeStruct((b, t, wd), F32)] * 4
        + [jax.ShapeDtypeStruct((b, 2, heads, n, n), F32)],
        scratch_shapes=[pltpu.VMEM((2, heads, n, n), F32)],
        compiler_params=_cparams(("parallel", "arbitrary")),
    )(u, u, s0, *args)


def _proj_out_kernel(x_ref, mod_ref, of_ref, ob_ref, gg_ref, gn_ref, yf_ref, yb_ref, bonus_ref, gate_ref,
                     lng_ref, lnb_ref, bmean_ref, w_ref, o_ref, *, gla_heads):
    m = mod_ref[0]
    o = of_ref[0, 0] + ob_ref[0, 0]
    gw = o.shape[-1]
    dv = gw // gla_heads
    parts = []
    for h in range(gla_heads):
        oh = o[:, h * dv:(h + 1) * dv]
        parts.append(oh * lax.rsqrt(jnp.mean(oh * oh, axis=-1, keepdims=True) + EPS))
    o_gla = jnp.concatenate(parts, axis=-1) * gn_ref[...] * _silu(gg_ref[0])
    y = yf_ref[0] + yb_ref[0]
    bmean = bmean_ref[...]
    dy = y - _dot(y, bmean, precision=HI)
    var = _dot(dy * dy, bmean, precision=HI)
    yn = dy * lax.rsqrt(var + RWKV_LN_EPS) * lng_ref[...] + lnb_ref[...]
    o_rwkv = (yn + bonus_ref[0]) * gate_ref[0]
    w = w_ref[...]
    mix = _bdot(o_gla, w[:gw]) + _bdot(o_rwkv, w[gw:])
    o_ref[0] = x_ref[0] + m[2:3] * mix


def proj_out(x, mod, mod_row, o_gla2, u_gla, gla_norm, yf, yb, bonus, gate, ln_g, ln_b, w_out, gla_heads, rwkv_n,
             tm=256):
    b, t, d = x.shape
    gw = o_gla2.shape[-1]
    rw = yf.shape[-1]
    tm = min(tm, t)
    bmean = np.kron(np.eye(rw // rwkv_n, dtype=np.float32), np.full((rwkv_n, rwkv_n), 1.0 / rwkv_n, np.float32))
    tile = lambda wdt: pl.BlockSpec((1, tm, wdt), lambda i, j: (i, j, 0))
    row = lambda wdt: pl.BlockSpec((1, wdt), lambda i, j: (0, 0))
    g_col_block = (u_gla.shape[-1] - LANES - gw) // gw
    return pl.pallas_call(
        functools.partial(_proj_out_kernel, gla_heads=gla_heads),
        grid=(b, t // tm),
        in_specs=[tile(d),
                  pl.BlockSpec((1, 6, d), lambda i, j: (mod_row(i), 0, 0)),
                  pl.BlockSpec((1, 1, tm, gw), lambda i, j: (0, i, j, 0)),
                  pl.BlockSpec((1, 1, tm, gw), lambda i, j: (1, i, j, 0)),
                  pl.BlockSpec((1, tm, gw), lambda i, j: (i, j, g_col_block)),
                  row(gw), tile(rw), tile(rw), tile(rw), tile(rw), row(rw), row(rw),
                  pl.BlockSpec((rw, rw), lambda i, j: (0, 0)),
                  pl.BlockSpec(w_out.shape, lambda i, j: (0, 0))],
        out_specs=tile(d),
        out_shape=jax.ShapeDtypeStruct((b, t, d), F32),
        compiler_params=_cparams(("parallel", "parallel")),
    )(x, mod, o_gla2, o_gla2, u_gla, gla_norm.reshape(1, gw), yf, yb, bonus, gate, ln_g.reshape(1, rw),
      ln_b.reshape(1, rw), jnp.asarray(bmean), w_out)


def _route_kernel(x_ref, mod_ref, g_ref, rw_ref, rb_ref, h_ref, gates_ref, *, n_exp):
    m = mod_ref[0]
    h = _rms_mod(x_ref[...], g_ref[...], m[4:5], m[3:4])
    h_ref[...] = h.astype(BF16)
    tm = h.shape[0]
    gsz = n_exp // N_GROUPS
    scores = _sigmoid(_dot(rw_ref[...], h, NT, precision=HI))
    sel = scores + rb_ref[...]
    neg = jnp.float32(-jnp.inf)
    s3 = sel.reshape(N_GROUPS, gsz, tm)
    idx = lax.broadcasted_iota(jnp.int32, s3.shape, 1)
    m1 = jnp.max(s3, axis=1, keepdims=True)
    first = jnp.min(jnp.where(s3 == m1, idx, gsz), axis=1, keepdims=True)
    m2 = jnp.max(jnp.where(idx == first, neg, s3), axis=1, keepdims=True)
    gs = (m1 + m2).reshape(N_GROUPS, tm)
    gidx = lax.broadcasted_iota(jnp.int32, gs.shape, 0)
    cnt = jnp.zeros(gs.shape, jnp.int32)
    for g in range(N_GROUPS):
        other = gs[g:g + 1]
        cnt = cnt + ((other > gs) | ((other == gs) & (g < gidx))).astype(jnp.int32)
    gkeep = jnp.broadcast_to((cnt < TOPK_GROUPS)[:, None, :], s3.shape).reshape(n_exp, tm)
    selm = jnp.where(gkeep, sel, neg)
    eidx = lax.broadcasted_iota(jnp.int32, selm.shape, 0)
    rank = jnp.zeros(selm.shape, jnp.int32)
    for e in range(n_exp):
        other = selm[e:e + 1]
        rank = rank + ((other > selm) | ((other == selm) & (e < eidx))).astype(jnp.int32)
    wts = jnp.where(rank < TOP_K, scores, 0.0)
    wts = wts / jnp.sum(wts, axis=0, keepdims=True) * ROUTED_SCALE
    pad = jnp.zeros((gates_ref.shape[-1] - n_exp, tm), F32)
    gates_ref[...] = jnp.concatenate([wts, pad], axis=0).T


def route(x2, mod, mod_row, norm_g, router_w, router_b, tm=256):
    nt, d = x2.shape
    n_exp = router_w.shape[1]
    tm = min(tm, nt)
    return pl.pallas_call(
        functools.partial(_route_kernel, n_exp=n_exp),
        grid=(nt // tm,),
        in_specs=[pl.BlockSpec((tm, d), lambda i: (i, 0)),
                  pl.BlockSpec((1, 6, d), lambda i: (mod_row(i, tm), 0, 0)),
                  pl.BlockSpec((1, d), lambda i: (0, 0)),
                  pl.BlockSpec((n_exp, d), lambda i: (0, 0)),
                  pl.BlockSpec((n_exp, 1), lambda i: (0, 0))],
        out_specs=[pl.BlockSpec((tm, d), lambda i: (i, 0)),
                   pl.BlockSpec((tm, LANES), lambda i: (i, 0))],
        out_shape=[jax.ShapeDtypeStruct((nt, d), BF16),
                   jax.ShapeDtypeStruct((nt, LANES), F32)],
        compiler_params=_cparams(("parallel",)),
    )(x2, mod, norm_g.reshape(1, d), router_w.T, router_b.reshape(n_exp, 1))


def _swiglu(h, w1, w3, w2, gate=None):
    mid = _silu(_bdot(h, w1)) * _bdot(h, w3)
    if gate is not None:
        mid = mid * gate
    return _bdot(mid, w2)


def _moe_kernel(x_ref, mod_ref, h_ref, gates_ref, w1_ref, w3_ref, w2_ref, s1_ref, s3_ref, s2_ref, o_ref, acc_ref):
    e = pl.program_id(1)
    h = h_ref[...]

    @pl.when(e == 0)
    def _():
        acc_ref[...] = _swiglu(h, s1_ref[...], s3_ref[...], s2_ref[...])

    ff = w1_ref.shape[-1]
    pick = (lax.broadcasted_iota(jnp.int32, (gates_ref.shape[-1], ff), 0) == e).astype(F32)
    gate = _dot(gates_ref[...], pick, precision=HI)
    acc_ref[...] += _swiglu(h, w1_ref[0], w3_ref[0], w2_ref[0], gate)

    @pl.when(e == pl.num_programs(1) - 1)
    def _():
        o_ref[...] = x_ref[...] + mod_ref[0][5:6] * acc_ref[...]


def moe(x2, mod, mod_row, h, gates, w1, w3, w2, s1, s3, s2, tm=1024):
    nt, d = x2.shape
    n_exp, _, ff = w1.shape
    tm = min(tm, nt)
    full = lambda a: pl.BlockSpec(a.shape, lambda i, e: (0,) * a.ndim)
    return pl.pallas_call(
        _moe_kernel,
        grid=(nt // tm, n_exp),
        in_specs=[pl.BlockSpec((tm, d), lambda i, e: (i, 0)),
                  pl.BlockSpec((1, 6, d), lambda i, e: (mod_row(i, tm), 0, 0)),
                  pl.BlockSpec((tm, d), lambda i, e: (i, 0)),
                  pl.BlockSpec((tm, LANES), lambda i, e: (i, 0)),
                  pl.BlockSpec((1, d, ff), lambda i, e: (e, 0, 0)),
                  pl.BlockSpec((1, d, ff), lambda i, e: (e, 0, 0)),
                  pl.BlockSpec((1, ff, d), lambda i, e: (e, 0, 0)),
                  full(s1), full(s3), full(s2)],
        out_specs=pl.BlockSpec((tm, d), lambda i, e: (i, 0)),
        out_shape=jax.ShapeDtypeStruct((nt, d), F32),
        scratch_shapes=[pltpu.VMEM((tm, d), F32)],
        compiler_params=_cparams(("parallel", "arbitrary")),
    )(x2, mod, h, gates, w1, w3, w2, s1, s3, s2)


def _final_kernel(x_ref, g_ref, o_ref):
    x = x_ref[...]
    o_ref[...] = x * lax.rsqrt(jnp.mean(x * x, axis=-1, keepdims=True) + EPS) * g_ref[...]


def final_norm(x2, g, tm=512):
    nt, d = x2.shape
    tm = min(tm, nt)
    return pl.pallas_call(
        _final_kernel,
        grid=(nt // tm,),
        in_specs=[pl.BlockSpec((tm, d), lambda i: (i, 0)), pl.BlockSpec((1, d), lambda i: (0, 0))],
        out_specs=pl.BlockSpec((tm, d), lambda i: (i, 0)),
        out_shape=jax.ShapeDtypeStruct((nt, d), F32),
        compiler_params=_cparams(("parallel",)),
    )(x2, g.reshape(1, d))


def _pad_cols(a, n):
    return jnp.pad(a, [(0, 0)] * (a.ndim - 1) + [(0, n - a.shape[-1])])


def _round_up(n, m):
    return -(-n // m) * m


def _layer(x, mod, latent, s_gla_t, s_rwkv, p, dims, grid_w):
    b, t, d = x.shape
    ctx_row = mod.shape[0] - 1
    brow = (lambda i: i) if latent else (lambda i: ctx_row)
    trow = (lambda i, tm: (i * tm) // t) if latent else (lambda i, tm: ctx_row)
    u_gla, u_rwkv = proj_in(x, mod, brow, p['norm1'], p['w_in'], dims['gla_cols'])
    hh, ww = (t // grid_w, grid_w) if latent else (1, t)
    u_rwkv = conv(u_rwkv, p['rwkv_conv'], hh, ww)
    o_gla2, sg = gla(u_gla, s_gla_t, p['gla_w_dec'], p['gla_b_dec'], dims['gla_heads'], dims['gla_dk'], dims['gla_dv'])
    yf, yb, bonus, gate, sr = rwkv(u_rwkv, s_rwkv, p['rwkv_w2'], p['rwkv_w0'], p['rwkv_a2'], p['rwkv_a0'],
                                   p['rwkv_g2'], p['rwkv_k_k'], p['rwkv_k_a'], p['rwkv_r_k'],
                                   dims['rwkv_heads'], dims['rwkv_n'])
    x = proj_out(x, mod, brow, o_gla2, u_gla, p['gla_norm'], yf, yb, bonus, gate, p['rwkv_ln_g'], p['rwkv_ln_b'],
                 p['w_out'], dims['gla_heads'], dims['rwkv_n'])
    x2 = x.reshape(b * t, d)
    h, gates = route(x2, mod, trow, p['norm2'], p['router_w'], p['router_b'], tm=min(256, t))
    x2 = moe(x2, mod, trow, h, gates, p['exp_w1'], p['exp_w3'], p['exp_w2'], p['sh_w1'], p['sh_w3'], p['sh_w2'],
             tm=min(1024, t))
    return x2.reshape(b, t, d), sg, sr


def kernel(x_prompt, x_sample, c, state_gla, state_rwkv, c_ctx, ada_w, ada_b, norm1, norm2, norm_f, w_in, w_out,
           gla_w_dec, gla_b_dec, gla_norm, rwkv_conv, rwkv_w2, rwkv_w0, rwkv_a2, rwkv_a0, rwkv_g2, rwkv_k_k, rwkv_k_a,
           rwkv_r_k, rwkv_ln_g, rwkv_ln_b, router_w, router_b, exp_w1, exp_w3, exp_w2, sh_w1, sh_w3, sh_w2):
    depth, d, _ = ada_w.shape
    bp = x_prompt.shape[0]
    bs = x_sample.shape[0]
    grid_w = 64
    gla_heads, gla_dk, gla_dv = state_gla.shape[3:]
    rwkv_heads, rwkv_n = state_rwkv.shape[3:5]
    gla_qk, gla_w, rwkv_w = gla_heads * gla_dk, gla_heads * gla_dv, rwkv_heads * rwkv_n
    gla_cols_raw = 2 * gla_qk + 2 * gla_w + 2 * gla_w_dec.shape[2]
    rwkv_cols_raw = w_in.shape[2] - gla_cols_raw
    gla_cols = 2 * gla_qk + 2 * gla_w + LANES
    rwkv_cols = _round_up(rwkv_cols_raw, LANES)
    dims = dict(gla_cols=gla_cols, gla_heads=gla_heads, gla_dk=gla_dk, gla_dv=gla_dv,
                rwkv_heads=rwkv_heads, rwkv_n=rwkv_n)

    w_in_p = jnp.concatenate([_pad_cols(w_in[..., :gla_cols_raw], gla_cols),
                              _pad_cols(w_in[..., gla_cols_raw:], rwkv_cols)], axis=-1).astype(BF16)
    conv_p = _pad_cols(rwkv_conv, rwkv_cols)
    w_out_b = w_out.astype(BF16)
    e1, e3, e2 = exp_w1.astype(BF16), exp_w3.astype(BF16), exp_w2.astype(BF16)
    s1, s3, s2 = sh_w1.astype(BF16), sh_w3.astype(BF16), sh_w2.astype(BF16)

    rows = _round_up(bs + 1, 8)
    cvec = jnp.concatenate([c, c_ctx[None], jnp.zeros((rows - bs - 1, d), F32)], axis=0)
    mod_all = adaln(cvec, ada_w, ada_b).reshape(depth, rows, 6, d)[:, :bs + 1]

    zero_gla = jnp.zeros((bp, 2, gla_heads, gla_dv, gla_dk), F32)
    zero_rwkv = jnp.zeros((bp, 2, rwkv_heads, rwkv_n, rwkv_n), F32)
    state_gla_t = jnp.swapaxes(state_gla, -1, -2)

    xp, xs = x_prompt, x_sample
    new_gla, new_rwkv = [], []
    for l in range(depth):
        p = dict(norm1=norm1[l], norm2=norm2[l], w_in=w_in_p[l], w_out=w_out_b[l], gla_w_dec=gla_w_dec[l],
                 gla_b_dec=gla_b_dec[l], gla_norm=gla_norm[l], rwkv_conv=conv_p[l], rwkv_w2=rwkv_w2[l],
                 rwkv_w0=rwkv_w0[l], rwkv_a2=rwkv_a2[l], rwkv_a0=rwkv_a0[l], rwkv_g2=rwkv_g2[l],
                 rwkv_k_k=rwkv_k_k[l], rwkv_k_a=rwkv_k_a[l], rwkv_r_k=rwkv_r_k[l], rwkv_ln_g=rwkv_ln_g[l],
                 rwkv_ln_b=rwkv_ln_b[l], router_w=router_w[l], router_b=router_b[l], exp_w1=e1[l], exp_w3=e3[l],
                 exp_w2=e2[l], sh_w1=s1[l], sh_w3=s3[l], sh_w2=s2[l])
        mod = mod_all[l]
        xp, sg, sr = _layer(xp, mod, False, zero_gla, zero_rwkv, p, dims, grid_w)
        new_gla.append(jnp.swapaxes(sg, -1, -2))
        new_rwkv.append(sr)
        xs, _, _ = _layer(xs, mod, True, state_gla_t[:, l], state_rwkv[:, l], p, dims, grid_w)
    y_prompt = final_norm(xp.reshape(-1, d), norm_f).reshape(xp.shape)
    y_sample = final_norm(xs.reshape(-1, d), norm_f).reshape(xs.shape)
    return (y_prompt, y_sample, jnp.stack(new_gla, axis=1), jnp.stack(new_rwkv, axis=1))
```

```python
import functools

import numpy as np
import jax
import jax.numpy as jnp
from jax import lax
from jax.experimental import pallas as pl
from jax.experimental.pallas import tpu as pltpu

F32 = jnp.float32
BF16 = jnp.bfloat16
HI = lax.Precision.HIGHEST

LANES = 128
EPS = 1e-6
RWKV_LN_EPS = 64e-5
GLA_NORMALIZER = 16.0
ROUTED_SCALE = 2.5
N_GROUPS = 8
TOPK_GROUPS = 4
TOP_K = 8
CHUNK = 64
VMEM_LIMIT = 56 * 1024 * 1024


def _cparams(sem):
    return pltpu.CompilerParams(dimension_semantics=sem, vmem_limit_bytes=VMEM_LIMIT)


def _dot(a, b, dims=(((1,), (0,)), ((), ())), precision=None):
    return lax.dot_general(a, b, dims, precision=precision, preferred_element_type=F32)


def _bdot(a, b, dims=(((1,), (0,)), ((), ()))):
    return lax.dot_general(a.astype(BF16), b.astype(BF16), dims, preferred_element_type=F32)


NT = (((1,), (1,)), ((), ()))
TN = (((0,), (0,)), ((), ()))
BNN = (((2,), (1,)), ((0,), (0,)))
BNT = (((2,), (2,)), ((0,), (0,)))


def _bmm(a, b, dims=BNN):
    return lax.dot_general(a.astype(BF16), b.astype(BF16), dims, preferred_element_type=F32)


def _two_terms(x):
    hi = x.astype(BF16)
    return hi, (x - hi.astype(F32)).astype(BF16)


def _split_dot(x, m):
    hi, lo = _two_terms(x)
    mb = m.astype(BF16)
    return _dot(hi, mb) + _dot(lo, mb)


def _split_dot_l(m, x):
    hi, lo = _two_terms(x)
    mb = m.astype(BF16)
    return _dot(mb, hi) + _dot(mb, lo)


def _sigmoid(x):
    return 1.0 / (1.0 + jnp.exp(-x))


def _silu(x):
    return x * _sigmoid(x)


def _softplus(x):
    return jnp.maximum(x, 0.0) + jnp.log(1.0 + jnp.exp(-jnp.abs(x)))


def _log_sigmoid(x):
    return -_softplus(-x)


def _rms_mod(x, g, scale, shift):
    y = x * lax.rsqrt(jnp.mean(x * x, axis=-1, keepdims=True) + EPS)
    return (y * g) * (1.0 + scale) + shift


def _adaln_kernel(c_ref, w_ref, b_ref, o_ref):
    c = c_ref[...]
    o_ref[0] = _dot(_silu(c), w_ref[0], precision=HI) + b_ref[0]


def adaln(cvec, ada_w, ada_b, tn=512):
    depth, d, n = ada_w.shape
    r = cvec.shape[0]
    return pl.pallas_call(
        _adaln_kernel,
        grid=(depth, n // tn),
        in_specs=[pl.BlockSpec((r, d), lambda l, j: (0, 0)),
                  pl.BlockSpec((1, d, tn), lambda l, j: (l, 0, j)),
                  pl.BlockSpec((1, 1, tn), lambda l, j: (l, 0, j))],
        out_specs=pl.BlockSpec((1, r, tn), lambda l, j: (l, 0, j)),
        out_shape=jax.ShapeDtypeStruct((depth, r, n), F32),
        compiler_params=_cparams(("parallel", "parallel")),
    )(cvec, ada_w, ada_b.reshape(depth, 1, n))


def _proj_in_kernel(x_ref, mod_ref, g_ref, w_ref, og_ref, or_ref, *, n_gla):
    m = mod_ref[0]
    h = _rms_mod(x_ref[0], g_ref[...], m[1:2], m[0:1])
    u = _bdot(h, w_ref[...])
    og_ref[0] = u[:, :n_gla]
    or_ref[0] = u[:, n_gla:]


def proj_in(x, mod, mod_row, norm_g, w, n_gla, tm=256):
    b, t, d = x.shape
    n = w.shape[1]
    tm = min(tm, t)
    return pl.pallas_call(
        functools.partial(_proj_in_kernel, n_gla=n_gla),
        grid=(b, t // tm),
        in_specs=[pl.BlockSpec((1, tm, d), lambda i, j: (i, j, 0)),
                  pl.BlockSpec((1, 6, d), lambda i, j: (mod_row(i), 0, 0)),
                  pl.BlockSpec((1, d), lambda i, j: (0, 0)),
                  pl.BlockSpec((d, n), lambda i, j: (0, 0))],
        out_specs=[pl.BlockSpec((1, tm, n_gla), lambda i, j: (i, j, 0)),
                   pl.BlockSpec((1, tm, n - n_gla), lambda i, j: (i, j, 0))],
        out_shape=[jax.ShapeDtypeStruct((b, t, n_gla), F32),
                   jax.ShapeDtypeStruct((b, t, n - n_gla), F32)],
        compiler_params=_cparams(("parallel", "parallel")),
    )(x, mod, norm_g.reshape(1, d), w)


CONV_PAD = 8


def _conv_kernel(u_ref, w_ref, o_ref, pad_ref, *, hh, ww):
    pad_ref[...] = jnp.zeros(pad_ref.shape, F32)
    pad_ref[1:hh + 1, CONV_PAD:CONV_PAD + ww, :] = u_ref[0]
    w = w_ref[...]
    acc = jnp.zeros((hh, ww, u_ref.shape[-1]), F32)
    for dy in range(3):
        if hh == 1 and dy != 1:
            continue
        for dx in range(3):
            acc = acc + pad_ref[dy:dy + hh, CONV_PAD - 1 + dx:CONV_PAD - 1 + dx + ww, :] * w[dy, dx]
    o_ref[0] = acc


def conv(u, wconv, hh, ww):
    b, t, c = u.shape
    out = pl.pallas_call(
        functools.partial(_conv_kernel, hh=hh, ww=ww),
        grid=(b, c // LANES),
        in_specs=[pl.BlockSpec((1, hh, ww, LANES), lambda i, j: (i, 0, 0, j)),
                  pl.BlockSpec((3, 3, LANES), lambda i, j: (0, 0, j))],
        out_specs=pl.BlockSpec((1, hh, ww, LANES), lambda i, j: (i, 0, 0, j)),
        out_shape=jax.ShapeDtypeStruct((b, hh, ww, c), F32),
        scratch_shapes=[pltpu.VMEM((hh + 2, ww + 2 * CONV_PAD, LANES), F32)],
        compiler_params=_cparams(("parallel", "parallel")),
    )(u.reshape(b, hh, ww, c), wconv)
    return out.reshape(b, t, c)


GLA_LEVELS = (32, 16, 8, 4, 2, 1)


def _order_consts(rev):
    c = CHUNK
    idx = np.arange(c)
    pos = (c - 1 - idx) if rev else idx
    incl = (pos[None, :] <= pos[:, None])
    strict = (pos[None, :] < pos[:, None])
    return pos, incl, strict


def _gla_consts():
    c = CHUNK
    mats, masks = [], []
    for rev in (False, True):
        pos, incl, strict = _order_consts(rev)
        later = (pos[None, :] > pos[:, None])
        rows = [incl, later]
        pins, sxs, lm = [], [], [np.eye(c, dtype=bool)]
        for s in GLA_LEVELS:
            blk = pos // s
            same = blk[None, :] == blk[:, None]
            pins.append(same & incl)
            sxs.append(same & later)
            lm.append((blk[:, None] % 2 == 1) & (blk[None, :] == blk[:, None] - 1))
        mats.append(np.concatenate(rows + pins + sxs, axis=0).astype(np.float32))
        masks.append(np.stack(lm).astype(np.float32))
    return np.stack(mats), np.stack(masks)


def _rwkv_consts():
    c = CHUNK
    out = []
    for rev in (False, True):
        pos, incl, strict = _order_consts(rev)
        b16 = pos // 16
        b32 = pos // 32
        same16 = b16[None, :] == b16[:, None]
        same32 = b32[None, :] == b32[:, None]
        out.append(np.stack([incl, strict, strict & same16, strict & same32 & ~same16,
                             strict & ~same32, np.eye(c, dtype=bool)]).astype(np.float32))
    return np.stack(out)


def _gla_kernel(u_ref, s0_ref, cst_ref, msk_ref, hm_ref, wd_ref, bd_ref, o_ref, sf_ref, s_scr, *, heads, dk, dv):
    ci = pl.program_id(2)

    @pl.when(ci == 0)
    def _():
        s_scr[...] = s0_ref[0, 0]

    c = CHUNK
    qk = heads * dk
    u = u_ref[0]
    q = u[:, 0:qk] * (dk ** -0.5)
    k = u[:, qk:2 * qk]
    v = u[:, 2 * qk:2 * qk + heads * dv]
    z = u[:, 2 * qk + 2 * heads * dv:]
    gk = _log_sigmoid(_dot(z, wd_ref[0], precision=HI) + bd_ref[0]) * (1.0 / GLA_NORMALIZER)
    e = jnp.exp(_split_dot_l(cst_ref[0], gk))
    e_cum, e_rest = e[0:c], e[c:2 * c]
    nl = len(GLA_LEVELS)
    msk = msk_ref[0]
    hm = hm_ref[...]

    def per_head(x):
        return (jnp.concatenate([x] * heads, axis=0) * hm).astype(BF16)

    attn = _dot(per_head(q), k.astype(BF16), NT) * msk[0]
    for li in range(nl):
        qs = q * e[(2 + li) * c:(3 + li) * c]
        ks = k * e[(2 + nl + li) * c:(3 + nl + li) * c]
        attn = attn + _dot(per_head(qs), ks.astype(BF16), NT) * msk[1 + li]
    st = s_scr[...]
    o_inter = _dot(per_head(q * e_cum), st.astype(BF16), NT)
    attn = attn.astype(BF16)
    vb = v.astype(BF16)
    outs = []
    for h in range(heads):
        outs.append(o_inter[h * c:(h + 1) * c] + _dot(attn[h * c:(h + 1) * c], vb[:, h * dv:(h + 1) * dv]))
    o_ref[0, 0] = jnp.concatenate(outs, axis=-1)
    etot = jnp.exp(jnp.sum(gk, axis=0, keepdims=True))
    vstack = jnp.concatenate([vb[:, h * dv:(h + 1) * dv] for h in range(heads)], axis=0)
    s_scr[...] = st * etot + _dot(vstack, per_head(k * e_rest), TN)

    @pl.when(ci == pl.num_programs(2) - 1)
    def _():
        sf_ref[0, 0] = s_scr[...]


def gla(u_gla, s0c, w_dec, b_dec, heads, dk, dv):
    b, t, cols = u_gla.shape
    nc = t // CHUNK
    qk = heads * dk
    rank = w_dec.shape[1]
    cst, msk = _gla_consts()
    msk = np.tile(msk, (1, 1, heads, 1))
    hm = np.kron(np.eye(heads, dtype=np.float32), np.ones((CHUNK, dk), np.float32))
    wd = jnp.zeros((2, LANES, qk), F32)
    wd = wd.at[0, 0:rank].set(w_dec[0]).at[1, rank:2 * rank].set(w_dec[1])

    def tok(i, d, j):
        return (i, j + d * (nc - 1 - 2 * j), 0)

    return pl.pallas_call(
        functools.partial(_gla_kernel, heads=heads, dk=dk, dv=dv),
        grid=(b, 2, nc),
        in_specs=[pl.BlockSpec((1, CHUNK, cols), tok),
                  pl.BlockSpec((1, 1, dv, qk), lambda i, d, j: (i, d, 0, 0)),
                  pl.BlockSpec((1,) + cst.shape[1:], lambda i, d, j: (d, 0, 0)),
                  pl.BlockSpec((1,) + msk.shape[1:], lambda i, d, j: (d, 0, 0, 0)),
                  pl.BlockSpec(hm.shape, lambda i, d, j: (0, 0)),
                  pl.BlockSpec((1, LANES, qk), lambda i, d, j: (d, 0, 0)),
                  pl.BlockSpec((1, 1, qk), lambda i, d, j: (d, 0, 0))],
        out_specs=[pl.BlockSpec((1, 1, CHUNK, heads * dv), lambda i, d, j: (d,) + tok(i, d, j)),
                   pl.BlockSpec((1, 1, dv, qk), lambda i, d, j: (i, d, 0, 0))],
        out_shape=[jax.ShapeDtypeStruct((2, b, t, heads * dv), F32),
                   jax.ShapeDtypeStruct((b, 2, dv, qk), F32)],
        scratch_shapes=[pltpu.VMEM((dv, qk), F32)],
        compiler_params=_cparams(("parallel", "parallel", "arbitrary")),
    )(u_gla, s0c, jnp.asarray(cst), jnp.asarray(msk), jnp.asarray(hm), wd, b_dec.reshape(2, 1, qk))


def _gla_state_in(s):
    h, dk, dv = s.shape[-3:]
    return jnp.moveaxis(s, -1, -3).reshape(s.shape[:-3] + (dv, h * dk))


def _gla_state_out(s, heads):
    dv, qk = s.shape[-2:]
    return jnp.moveaxis(s.reshape(s.shape[:-2] + (dv, heads, qk // heads)), -3, -1)


def _tri_inverse(l, msk):
    eye = msk[5]
    ld = l * msk[2]
    x = eye - ld
    p = _bmm(ld, ld)
    x = x + _bmm(x, p)
    p = _bmm(p, p)
    x = x + _bmm(x, p)
    p = _bmm(p, p)
    x = x + _bmm(x, p)
    for mi in (3, 4):
        x = x - _bmm(x, _bmm(l * msk[mi], x))
    return x


def _rwkv_kernel(uf_ref, ub_ref, s0_ref, msk_ref, w2_ref, w0_ref, a2_ref, a0_ref, g2_ref, kk_ref, ka_ref, rk_ref,
                 bones_ref, yf_ref, yb_ref, bonus_ref, gate_ref, sf_ref, s_scr, *, heads, n):
    ci = pl.program_id(1)

    @pl.when(ci == 0)
    def _():
        s_scr[...] = s0_ref[0]

    c = CHUNK
    wd = heads * n
    bones = bones_ref[...]

    def split_heads(x):
        return jnp.stack([x[:, h * n:(h + 1) * n] for h in range(heads)], axis=0)

    for d, u_ref in ((0, uf_ref), (1, ub_ref)):
        u = u_ref[0]
        msk = msk_ref[d]
        r, k, v = u[:, 0:wd], u[:, wd:2 * wd], u[:, 2 * wd:3 * wd]
        z0 = 3 * wd
        zw = u[:, z0 + d * n:z0 + (d + 1) * n]
        za = u[:, z0 + 2 * n:z0 + 3 * n]
        wlog = -_softplus(-(w0_ref[d] + _dot(jnp.tanh(zw), w2_ref[d], precision=HI))) - 0.5
        logw = -jnp.exp(wlog)
        a = _sigmoid(a0_ref[...] + _dot(za, a2_ref[...], precision=HI))
        kk = k * kk_ref[...]
        kk = kk * lax.rsqrt(jnp.maximum(_split_dot(kk * kk, bones), 1e-24))
        km = k * (1.0 + (a - 1.0) * ka_ref[...])
        bb = kk * a
        cum = _split_dot_l(msk[0], logw)
        tot = jnp.sum(logw, axis=0, keepdims=True)
        e_neg = jnp.exp(-cum)
        e_rest = jnp.exp(tot - cum)
        e_tot = jnp.exp(tot)
        ar = split_heads(jnp.concatenate([kk * jnp.exp(cum - logw), r * jnp.exp(cum)], axis=0))
        kb = split_heads(jnp.concatenate([km * e_neg, bb * e_neg], axis=0))
        vv = split_heads(v)
        st = s_scr[d]
        sc = _bmm(ar, kb, BNT)
        strict = msk[1] > 0
        incl = msk[0] > 0
        lak = jnp.where(strict, sc[:, :c, :c], 0.0)
        lab = jnp.where(strict, sc[:, :c, c:], 0.0)
        mrk = jnp.where(incl, sc[:, c:, :c], 0.0)
        mrb = jnp.where(incl, sc[:, c:, c:], 0.0)
        through = _bmm(ar, st, BNT)
        rhs = through[:, :c] + _bmm(lak, vv)
        uu = _bmm(_tri_inverse(lab, msk), rhs)
        vu = jnp.concatenate([vv, -uu], axis=1)
        y = through[:, c:] + _bmm(jnp.concatenate([mrk, mrb], axis=2), vu)
        (yf_ref, yb_ref)[d][0] = jnp.concatenate([y[h] for h in range(heads)], axis=-1)
        kbh = jnp.concatenate([km * e_rest, bb * e_rest], axis=0)
        for h in range(heads):
            sl = slice(h * n, (h + 1) * n)
            s_scr[d, h] = st[h] * e_tot[:, sl] + _bdot(vu[h], kbh[:, sl], TN)
        if d == 0:
            bonus_ref[0] = _split_dot(r * km * rk_ref[...], bones) * v
            zg = u[:, z0 + 3 * n:z0 + 3 * n + g2_ref.shape[0]]
            gate_ref[0] = _dot(_sigmoid(zg), g2_ref[...], precision=HI)

    @pl.when(ci == pl.num_programs(1) - 1)
    def _():
        sf_ref[0] = s_scr[...]


def rwkv(u, s0, w2, w0, a2, a0, g2, k_k, k_a, r_k, heads, n):
    b, t, cols = u.shape
    nc = t // CHUNK
    wd = heads * n
    msk = _rwkv_consts()
    bones = np.kron(np.eye(heads, dtype=np.float32), np.ones((n, n), np.float32))
    row = lambda x: x.reshape(1, wd)
    full = lambda a: pl.BlockSpec(a.shape, lambda i, j: (0,) * a.ndim)
    args = [jnp.asarray(msk), w2, w0.reshape(2, 1, wd), a2, row(a0), g2, row(k_k), row(k_a), row(r_k),
            jnp.asarray(bones)]
    return pl.pallas_call(
        functools.partial(_rwkv_kernel, heads=heads, n=n),
        grid=(b, nc),
        in_specs=[pl.BlockSpec((1, CHUNK, cols), lambda i, j: (i, j, 0)),
                  pl.BlockSpec((1, CHUNK, cols), lambda i, j: (i, nc - 1 - j, 0)),
                  pl.BlockSpec((1, 2, heads, n, n), lambda i, j: (i, 0, 0, 0, 0))] + [full(a) for a in args],
        out_specs=[pl.BlockSpec((1, CHUNK, wd), lambda i, j: (i, j, 0)),
                   pl.BlockSpec((1, CHUNK, wd), lambda i, j: (i, nc - 1 - j, 0)),
                   pl.BlockSpec((1, CHUNK, wd), lambda i, j: (i, j, 0)),
                   pl.BlockSpec((1, CHUNK, wd), lambda i, j: (i, j, 0)),
                   pl.BlockSpec((1, 2, heads, n, n), lambda i, j: (i, 0, 0, 0, 0))],
        out_shape=[jax.ShapeDtypeStruct((b, t, wd), F32)] * 4
        + [jax.ShapeDtypeStruct((b, 2, heads, n, n), F32)],
        scratch_shapes=[pltpu.VMEM((2, heads, n, n), F32)],
        compiler_params=_cparams(("parallel", "arbitrary")),
    )(u, u, s0, *args)


def _proj_out_kernel(x_ref, mod_ref, of_ref, ob_ref, gg_ref, gn_ref, yf_ref, yb_ref, bonus_ref, gate_ref,
                     lng_ref, lnb_ref, bmean_ref, w_ref, o_ref, *, gla_heads):
    m = mod_ref[0]
    o = of_ref[0, 0] + ob_ref[0, 0]
    gw = o.shape[-1]
    dv = gw // gla_heads
    parts = []
    for h in range(gla_heads):
        oh = o[:, h * dv:(h + 1) * dv]
        parts.append(oh * lax.rsqrt(jnp.mean(oh * oh, axis=-1, keepdims=True) + EPS))
    o_gla = jnp.concatenate(parts, axis=-1) * gn_ref[...] * _silu(gg_ref[0])
    y = yf_ref[0] + yb_ref[0]
    bmean = bmean_ref[...]
    dy = y - _split_dot(y, bmean)
    var = _split_dot(dy * dy, bmean)
    yn = dy * lax.rsqrt(var + RWKV_LN_EPS) * lng_ref[...] + lnb_ref[...]
    o_rwkv = (yn + bonus_ref[0]) * gate_ref[0]
    w = w_ref[...]
    mix = _bdot(o_gla, w[:gw]) + _bdot(o_rwkv, w[gw:])
    o_ref[0] = x_ref[0] + m[2:3] * mix


def proj_out(x, mod, mod_row, o_gla2, u_gla, gla_norm, yf, yb, bonus, gate, ln_g, ln_b, w_out, gla_heads, rwkv_n,
             tm=256):
    b, t, d = x.shape
    gw = o_gla2.shape[-1]
    rw = yf.shape[-1]
    tm = min(tm, t)
    bmean = np.kron(np.eye(rw // rwkv_n, dtype=np.float32), np.full((rwkv_n, rwkv_n), 1.0 / rwkv_n, np.float32))
    tile = lambda wdt: pl.BlockSpec((1, tm, wdt), lambda i, j: (i, j, 0))
    row = lambda wdt: pl.BlockSpec((1, wdt), lambda i, j: (0, 0))
    g_col_block = (u_gla.shape[-1] - LANES - gw) // gw
    return pl.pallas_call(
        functools.partial(_proj_out_kernel, gla_heads=gla_heads),
        grid=(b, t // tm),
        in_specs=[tile(d),
                  pl.BlockSpec((1, 6, d), lambda i, j: (mod_row(i), 0, 0)),
                  pl.BlockSpec((1, 1, tm, gw), lambda i, j: (0, i, j, 0)),
                  pl.BlockSpec((1, 1, tm, gw), lambda i, j: (1, i, j, 0)),
                  pl.BlockSpec((1, tm, gw), lambda i, j: (i, j, g_col_block)),
                  row(gw), tile(rw), tile(rw), tile(rw), tile(rw), row(rw), row(rw),
                  pl.BlockSpec((rw, rw), lambda i, j: (0, 0)),
                  pl.BlockSpec(w_out.shape, lambda i, j: (0, 0))],
        out_specs=tile(d),
        out_shape=jax.ShapeDtypeStruct((b, t, d), F32),
        compiler_params=_cparams(("parallel", "parallel")),
    )(x, mod, o_gla2, o_gla2, u_gla, gla_norm.reshape(1, gw), yf, yb, bonus, gate, ln_g.reshape(1, rw),
      ln_b.reshape(1, rw), jnp.asarray(bmean), w_out)


MOE_SUB = 256
MOE_CAP = 64
MOE_SUPER = 1024


def _route_kernel(x_ref, mod_ref, g_ref, rw_ref, rb_ref, tri_ref, h_ref, gates_ref, pos_ref, *, n_exp):
    m = mod_ref[0]
    h = _rms_mod(x_ref[...], g_ref[...], m[4:5], m[3:4])
    h_ref[...] = h.astype(BF16)
    tm = h.shape[0]
    gsz = n_exp // N_GROUPS
    scores = _sigmoid(_dot(rw_ref[...], h, NT, precision=HI))
    sel = scores + rb_ref[...]
    neg = jnp.float32(-jnp.inf)
    s3 = sel.reshape(N_GROUPS, gsz, tm)
    idx = lax.broadcasted_iota(jnp.int32, s3.shape, 1)
    m1 = jnp.max(s3, axis=1, keepdims=True)
    first = jnp.min(jnp.where(s3 == m1, idx, gsz), axis=1, keepdims=True)
    m2 = jnp.max(jnp.where(idx == first, neg, s3), axis=1, keepdims=True)
    gs = (m1 + m2).reshape(N_GROUPS, tm)
    gidx = lax.broadcasted_iota(jnp.int32, gs.shape, 0)
    cnt = jnp.zeros(gs.shape, jnp.int32)
    for g in range(N_GROUPS):
        other = gs[g:g + 1]
        cnt = cnt + ((other > gs) | ((other == gs) & (g < gidx))).astype(jnp.int32)
    gkeep = jnp.broadcast_to((cnt < TOPK_GROUPS)[:, None, :], s3.shape).reshape(n_exp, tm)
    selm = jnp.where(gkeep, sel, neg)
    eidx = lax.broadcasted_iota(jnp.int32, selm.shape, 0)
    rank = jnp.zeros(selm.shape, jnp.int32)
    for e in range(n_exp):
        other = selm[e:e + 1]
        rank = rank + ((other > selm) | ((other == selm) & (e < eidx))).astype(jnp.int32)
    chosen = rank < TOP_K
    wts = jnp.where(chosen, scores, 0.0)
    gates_ref[...] = wts / jnp.sum(wts, axis=0, keepdims=True) * ROUTED_SCALE
    pos_ref[...] = _dot(chosen.astype(BF16), tri_ref[...])


def route(x2, mod, mod_row, norm_g, router_w, router_b, tm):
    nt, d = x2.shape
    n_exp = router_w.shape[1]
    tri = np.triu(np.ones((tm, tm), np.float32), 1)
    return pl.pallas_call(
        functools.partial(_route_kernel, n_exp=n_exp),
        grid=(nt // tm,),
        in_specs=[pl.BlockSpec((tm, d), lambda i: (i, 0)),
                  pl.BlockSpec((1, 6, d), lambda i: (mod_row(i, tm), 0, 0)),
                  pl.BlockSpec((1, d), lambda i: (0, 0)),
                  pl.BlockSpec((n_exp, d), lambda i: (0, 0)),
                  pl.BlockSpec((n_exp, 1), lambda i: (0, 0)),
                  pl.BlockSpec((tm, tm), lambda i: (0, 0))],
        out_specs=[pl.BlockSpec((tm, d), lambda i: (i, 0)),
                   pl.BlockSpec((n_exp, tm), lambda i: (0, i)),
                   pl.BlockSpec((n_exp, tm), lambda i: (0, i))],
        out_shape=[jax.ShapeDtypeStruct((nt, d), BF16),
                   jax.ShapeDtypeStruct((n_exp, nt), F32),
                   jax.ShapeDtypeStruct((n_exp, nt), F32)],
        compiler_params=_cparams(("parallel",)),
    )(x2, mod, norm_g.reshape(1, d), router_w.T, router_b.reshape(n_exp, 1), jnp.asarray(tri, BF16))


def _swiglu(h, w1, w3, w2):
    return _bdot(_silu(_bdot(h, w1)) * _bdot(h, w3), w2)


def _moe_kernel(npass_ref, x_ref, mod_ref, h_ref, gates_ref, pos_ref, w1_ref, w3_ref, w2_ref, s1_ref, s3_ref, s2_ref,
                o_ref, acc_ref, xe_ref, *, sub, cap):
    i = pl.program_id(0)
    e = pl.program_id(1)
    n_exp = pl.num_programs(1)
    n_sub = h_ref.shape[0] // sub

    @pl.when(e == 0)
    def _():
        acc_ref[...] = _swiglu(h_ref[...], s1_ref[...], s3_ref[...], s2_ref[...])

    gate = gates_ref[pl.ds(e, 1), :]
    pos = pos_ref[pl.ds(e, 1), :]
    rows = lax.broadcasted_iota(jnp.int32, (cap, sub), 0).astype(F32)

    def one_pass(it, carry):
        base = (it * cap).astype(F32)
        picks = []
        for s in range(n_sub):
            sl = slice(s * sub, (s + 1) * sub)
            g = gate[:, sl]
            pick = jnp.where((pos[:, sl] - base == rows) & (g > 0.0), g, 0.0)
            picks.append(pick.astype(BF16))
            onehot = (pick > 0.0).astype(BF16)
            xe_ref[s * cap:(s + 1) * cap, :] = _dot(onehot, h_ref[sl, :]).astype(BF16)
        ye = _swiglu(xe_ref[...], w1_ref[0], w3_ref[0], w2_ref[0]).astype(BF16)
        for s in range(n_sub):
            sl = slice(s * sub, (s + 1) * sub)
            acc_ref[sl, :] += _dot(picks[s], ye[s * cap:(s + 1) * cap], TN)
        return carry

    lax.fori_loop(0, npass_ref[i * n_exp + e], one_pass, 0)

    @pl.when(e == n_exp - 1)
    def _():
        o_ref[...] = x_ref[...] + mod_ref[0][5:6] * acc_ref[...]


def moe(x2, mod, mod_row, h, gates, pos, w1, w3, w2, s1, s3, s2, sub, tm):
    nt, d = x2.shape
    n_exp, _, ff = w1.shape
    cap = min(MOE_CAP, sub)
    n_sub = tm // sub
    cnt = (pos + (gates > 0.0))[:, sub - 1::sub].reshape(n_exp, nt // tm, n_sub).max(axis=-1)
    npass = ((cnt.astype(jnp.int32) + cap - 1) // cap).T.reshape(-1)
    full = lambda a: pl.BlockSpec(a.shape, lambda i, e, n: (0,) * a.ndim)
    grid_spec = pltpu.PrefetchScalarGridSpec(
        num_scalar_prefetch=1,
        grid=(nt // tm, n_exp),
        in_specs=[pl.BlockSpec((tm, d), lambda i, e, n: (i, 0)),
                  pl.BlockSpec((1, 6, d), lambda i, e, n: (mod_row(i, tm), 0, 0)),
                  pl.BlockSpec((tm, d), lambda i, e, n: (i, 0)),
                  pl.BlockSpec((n_exp, tm), lambda i, e, n: (0, i)),
                  pl.BlockSpec((n_exp, tm), lambda i, e, n: (0, i)),
                  pl.BlockSpec((1, d, ff), lambda i, e, n: (e, 0, 0)),
                  pl.BlockSpec((1, d, ff), lambda i, e, n: (e, 0, 0)),
                  pl.BlockSpec((1, ff, d), lambda i, e, n: (e, 0, 0)),
                  full(s1), full(s3), full(s2)],
        out_specs=pl.BlockSpec((tm, d), lambda i, e, n: (i, 0)),
        scratch_shapes=[pltpu.VMEM((tm, d), F32), pltpu.VMEM((n_sub * cap, d), BF16)])
    return pl.pallas_call(
        functools.partial(_moe_kernel, sub=sub, cap=cap),
        grid_spec=grid_spec,
        out_shape=jax.ShapeDtypeStruct((nt, d), F32),
        compiler_params=_cparams(("parallel", "arbitrary")),
    )(npass, x2, mod, h, gates, pos, w1, w3, w2, s1, s3, s2)


def _final_kernel(x_ref, g_ref, o_ref):
    x = x_ref[...]
    o_ref[...] = x * lax.rsqrt(jnp.mean(x * x, axis=-1, keepdims=True) + EPS) * g_ref[...]


def final_norm(x2, g, tm=512):
    nt, d = x2.shape
    tm = min(tm, nt)
    return pl.pallas_call(
        _final_kernel,
        grid=(nt // tm,),
        in_specs=[pl.BlockSpec((tm, d), lambda i: (i, 0)), pl.BlockSpec((1, d), lambda i: (0, 0))],
        out_specs=pl.BlockSpec((tm, d), lambda i: (i, 0)),
        out_shape=jax.ShapeDtypeStruct((nt, d), F32),
        compiler_params=_cparams(("parallel",)),
    )(x2, g.reshape(1, d))


def _pad_cols(a, n):
    return jnp.pad(a, [(0, 0)] * (a.ndim - 1) + [(0, n - a.shape[-1])])


def _round_up(n, m):
    return -(-n // m) * m


def _layer(x, mod, latent, s_gla_c, s_rwkv, p, dims, grid_w):
    b, t, d = x.shape
    ctx_row = mod.shape[0] - 1
    brow = (lambda i: i) if latent else (lambda i: ctx_row)
    trow = (lambda i, tm: (i * tm) // t) if latent else (lambda i, tm: ctx_row)
    u_gla, u_rwkv = proj_in(x, mod, brow, p['norm1'], p['w_in'], dims['gla_cols'])
    hh, ww = (t // grid_w, grid_w) if latent else (1, t)
    u_rwkv = conv(u_rwkv, p['rwkv_conv'], hh, ww)
    o_gla2, sg = gla(u_gla, s_gla_c, p['gla_w_dec'], p['gla_b_dec'], dims['gla_heads'], dims['gla_dk'], dims['gla_dv'])
    yf, yb, bonus, gate, sr = rwkv(u_rwkv, s_rwkv, p['rwkv_w2'], p['rwkv_w0'], p['rwkv_a2'], p['rwkv_a0'],
                                   p['rwkv_g2'], p['rwkv_k_k'], p['rwkv_k_a'], p['rwkv_r_k'],
                                   dims['rwkv_heads'], dims['rwkv_n'])
    x = proj_out(x, mod, brow, o_gla2, u_gla, p['gla_norm'], yf, yb, bonus, gate, p['rwkv_ln_g'], p['rwkv_ln_b'],
                 p['w_out'], dims['gla_heads'], dims['rwkv_n'])
    x2 = x.reshape(b * t, d)
    sub = min(MOE_SUB, t)
    tm = min(MOE_SUPER, t)
    h, gates, pos = route(x2, mod, trow, p['norm2'], p['router_w'], p['router_b'], sub)
    x2 = moe(x2, mod, trow, h, gates, pos, p['exp_w1'], p['exp_w3'], p['exp_w2'], p['sh_w1'], p['sh_w3'], p['sh_w2'],
             sub, tm)
    return x2.reshape(b, t, d), sg, sr


def kernel(x_prompt, x_sample, c, state_gla, state_rwkv, c_ctx, ada_w, ada_b, norm1, norm2, norm_f, w_in, w_out,
           gla_w_dec, gla_b_dec, gla_norm, rwkv_conv, rwkv_w2, rwkv_w0, rwkv_a2, rwkv_a0, rwkv_g2, rwkv_k_k, rwkv_k_a,
           rwkv_r_k, rwkv_ln_g, rwkv_ln_b, router_w, router_b, exp_w1, exp_w3, exp_w2, sh_w1, sh_w3, sh_w2):
    depth, d, _ = ada_w.shape
    bp = x_prompt.shape[0]
    bs = x_sample.shape[0]
    grid_w = 64
    gla_heads, gla_dk, gla_dv = state_gla.shape[3:]
    rwkv_heads, rwkv_n = state_rwkv.shape[3:5]
    gla_qk, gla_w, rwkv_w = gla_heads * gla_dk, gla_heads * gla_dv, rwkv_heads * rwkv_n
    gla_cols_raw = 2 * gla_qk + 2 * gla_w + 2 * gla_w_dec.shape[2]
    rwkv_cols_raw = w_in.shape[2] - gla_cols_raw
    gla_cols = 2 * gla_qk + 2 * gla_w + LANES
    rwkv_cols = _round_up(rwkv_cols_raw, LANES)
    dims = dict(gla_cols=gla_cols, gla_heads=gla_heads, gla_dk=gla_dk, gla_dv=gla_dv,
                rwkv_heads=rwkv_heads, rwkv_n=rwkv_n)

    w_in_p = jnp.concatenate([_pad_cols(w_in[..., :gla_cols_raw], gla_cols),
                              _pad_cols(w_in[..., gla_cols_raw:], rwkv_cols)], axis=-1).astype(BF16)
    conv_p = _pad_cols(rwkv_conv, rwkv_cols)
    w_out_b = w_out.astype(BF16)
    e1, e3, e2 = exp_w1.astype(BF16), exp_w3.astype(BF16), exp_w2.astype(BF16)
    s1, s3, s2 = sh_w1.astype(BF16), sh_w3.astype(BF16), sh_w2.astype(BF16)

    rows = _round_up(bs + 1, 8)
    cvec = jnp.concatenate([c, c_ctx[None], jnp.zeros((rows - bs - 1, d), F32)], axis=0)
    mod_all = adaln(cvec, ada_w, ada_b).reshape(depth, rows, 6, d)[:, :bs + 1]

    zero_gla = jnp.zeros((bp, 2, gla_dv, gla_qk), F32)
    zero_rwkv = jnp.zeros((bp, 2, rwkv_heads, rwkv_n, rwkv_n), F32)
    state_gla_c = _gla_state_in(state_gla)

    xp, xs = x_prompt, x_sample
    new_gla, new_rwkv = [], []
    for l in range(depth):
        p = dict(norm1=norm1[l], norm2=norm2[l], w_in=w_in_p[l], w_out=w_out_b[l], gla_w_dec=gla_w_dec[l],
                 gla_b_dec=gla_b_dec[l], gla_norm=gla_norm[l], rwkv_conv=conv_p[l], rwkv_w2=rwkv_w2[l],
                 rwkv_w0=rwkv_w0[l], rwkv_a2=rwkv_a2[l], rwkv_a0=rwkv_a0[l], rwkv_g2=rwkv_g2[l],
                 rwkv_k_k=rwkv_k_k[l], rwkv_k_a=rwkv_k_a[l], rwkv_r_k=rwkv_r_k[l], rwkv_ln_g=rwkv_ln_g[l],
                 rwkv_ln_b=rwkv_ln_b[l], router_w=router_w[l], router_b=router_b[l], exp_w1=e1[l], exp_w3=e3[l],
                 exp_w2=e2[l], sh_w1=s1[l], sh_w3=s3[l], sh_w2=s2[l])
        mod = mod_all[l]
        xp, sg, sr = _layer(xp, mod, False, zero_gla, zero_rwkv, p, dims, grid_w)
        new_gla.append(_gla_state_out(sg, gla_heads))
        new_rwkv.append(sr)
        xs, _, _ = _layer(xs, mod, True, state_gla_c[:, l], state_rwkv[:, l], p, dims, grid_w)
    y_prompt = final_norm(xp.reshape(-1, d), norm_f).reshape(xp.shape)
    y_sample = final_norm(xs.reshape(-1, d), norm_f).reshape(xs.shape)
    return (y_prompt, y_sample, jnp.stack(new_gla, axis=1), jnp.stack(new_rwkv, axis=1))
```

```python
import functools

import numpy as np
import jax
import jax.numpy as jnp
from jax import lax
from jax.experimental import pallas as pl
from jax.experimental.pallas import tpu as pltpu

F32 = jnp.float32
BF16 = jnp.bfloat16
HI = lax.Precision.HIGHEST

LANES = 128
EPS = 1e-6
RWKV_LN_EPS = 64e-5
GLA_NORMALIZER = 16.0
ROUTED_SCALE = 2.5
N_GROUPS = 8
TOPK_GROUPS = 4
TOP_K = 8
CHUNK = 64
VMEM_LIMIT = 56 * 1024 * 1024


def _cparams(sem):
    return pltpu.CompilerParams(dimension_semantics=sem, vmem_limit_bytes=VMEM_LIMIT)


def _dot(a, b, dims=(((1,), (0,)), ((), ())), precision=None):
    return lax.dot_general(a, b, dims, precision=precision, preferred_element_type=F32)


def _bdot(a, b, dims=(((1,), (0,)), ((), ()))):
    return lax.dot_general(a.astype(BF16), b.astype(BF16), dims, preferred_element_type=F32)


NT = (((1,), (1,)), ((), ()))
TN = (((0,), (0,)), ((), ()))
BNN = (((2,), (1,)), ((0,), (0,)))
BNT = (((2,), (2,)), ((0,), (0,)))


def _bmm(a, b, dims=BNN):
    return lax.dot_general(a.astype(BF16), b.astype(BF16), dims, preferred_element_type=F32)


def _two_terms(x):
    hi = x.astype(BF16)
    return hi, (x - hi.astype(F32)).astype(BF16)


def _split_dot(x, m):
    hi, lo = _two_terms(x)
    mb = m.astype(BF16)
    return _dot(hi, mb) + _dot(lo, mb)


def _split_dot_l(m, x):
    hi, lo = _two_terms(x)
    mb = m.astype(BF16)
    return _dot(mb, hi) + _dot(mb, lo)


def _sigmoid(x):
    return 1.0 / (1.0 + jnp.exp(-x))


def _silu(x):
    return x * _sigmoid(x)


def _softplus(x):
    return jnp.maximum(x, 0.0) + jnp.log(1.0 + jnp.exp(-jnp.abs(x)))


def _log_sigmoid(x):
    return -_softplus(-x)


def _rms_mod(x, g, scale, shift):
    y = x * lax.rsqrt(jnp.mean(x * x, axis=-1, keepdims=True) + EPS)
    return (y * g) * (1.0 + scale) + shift


def _adaln_kernel(c_ref, w_ref, b_ref, o_ref):
    c = c_ref[...]
    o_ref[0] = _dot(_silu(c), w_ref[0], precision=HI) + b_ref[0]


def adaln(cvec, ada_w, ada_b, tn=512):
    depth, d, n = ada_w.shape
    r = cvec.shape[0]
    return pl.pallas_call(
        _adaln_kernel,
        grid=(depth, n // tn),
        in_specs=[pl.BlockSpec((r, d), lambda l, j: (0, 0)),
                  pl.BlockSpec((1, d, tn), lambda l, j: (l, 0, j)),
                  pl.BlockSpec((1, 1, tn), lambda l, j: (l, 0, j))],
        out_specs=pl.BlockSpec((1, r, tn), lambda l, j: (l, 0, j)),
        out_shape=jax.ShapeDtypeStruct((depth, r, n), F32),
        compiler_params=_cparams(("parallel", "parallel")),
    )(cvec, ada_w, ada_b.reshape(depth, 1, n))


def _proj_in_kernel(x_ref, mod_ref, g_ref, w_ref, og_ref, or_ref, *, n_gla):
    m = mod_ref[0]
    h = _rms_mod(x_ref[0], g_ref[...], m[1:2], m[0:1])
    u = _bdot(h, w_ref[...])
    og_ref[0] = u[:, :n_gla]
    or_ref[0] = u[:, n_gla:]


def proj_in(x, mod, mod_row, norm_g, w, n_gla, tm=256):
    b, t, d = x.shape
    n = w.shape[1]
    tm = min(tm, t)
    return pl.pallas_call(
        functools.partial(_proj_in_kernel, n_gla=n_gla),
        grid=(b, t // tm),
        in_specs=[pl.BlockSpec((1, tm, d), lambda i, j: (i, j, 0)),
                  pl.BlockSpec((1, 6, d), lambda i, j: (mod_row(i), 0, 0)),
                  pl.BlockSpec((1, d), lambda i, j: (0, 0)),
                  pl.BlockSpec((d, n), lambda i, j: (0, 0))],
        out_specs=[pl.BlockSpec((1, tm, n_gla), lambda i, j: (i, j, 0)),
                   pl.BlockSpec((1, tm, n - n_gla), lambda i, j: (i, j, 0))],
        out_shape=[jax.ShapeDtypeStruct((b, t, n_gla), F32),
                   jax.ShapeDtypeStruct((b, t, n - n_gla), F32)],
        compiler_params=_cparams(("parallel", "parallel")),
    )(x, mod, norm_g.reshape(1, d), w)


CONV_PAD = 8


def _conv_kernel(u_ref, w_ref, o_ref, pad_ref, *, hh, ww, vertical):
    pad_ref[...] = jnp.zeros(pad_ref.shape, F32)
    pad_ref[1:hh + 1, CONV_PAD:CONV_PAD + ww, :] = u_ref[0]
    w = w_ref[...]
    acc = jnp.zeros((hh, ww, u_ref.shape[-1]), F32)
    for dy in range(3):
        if not vertical and dy != 1:
            continue
        for dx in range(3):
            acc = acc + pad_ref[dy:dy + hh, CONV_PAD - 1 + dx:CONV_PAD - 1 + dx + ww, :] * w[dy, dx]
    o_ref[0] = acc


def conv(u, wconv, hh, ww, vertical):
    c = u.shape[-1]
    b = u.size // (hh * ww * c)
    out = pl.pallas_call(
        functools.partial(_conv_kernel, hh=hh, ww=ww, vertical=vertical),
        grid=(b, c // LANES),
        in_specs=[pl.BlockSpec((1, hh, ww, LANES), lambda i, j: (i, 0, 0, j)),
                  pl.BlockSpec((3, 3, LANES), lambda i, j: (0, 0, j))],
        out_specs=pl.BlockSpec((1, hh, ww, LANES), lambda i, j: (i, 0, 0, j)),
        out_shape=jax.ShapeDtypeStruct((b, hh, ww, c), F32),
        scratch_shapes=[pltpu.VMEM((hh + 2, ww + 2 * CONV_PAD, LANES), F32)],
        compiler_params=_cparams(("parallel", "parallel")),
    )(u.reshape(b, hh, ww, c), wconv)
    return out.reshape(u.shape)


GLA_LEVELS = (32, 16, 8, 4, 2, 1)


def _order_consts(rev):
    c = CHUNK
    idx = np.arange(c)
    pos = (c - 1 - idx) if rev else idx
    incl = (pos[None, :] <= pos[:, None])
    strict = (pos[None, :] < pos[:, None])
    return pos, incl, strict


def _gla_consts():
    c = CHUNK
    mats, masks = [], []
    for rev in (False, True):
        pos, incl, strict = _order_consts(rev)
        later = (pos[None, :] > pos[:, None])
        rows = [incl, later]
        pins, sxs, lm = [], [], [np.eye(c, dtype=bool)]
        for s in GLA_LEVELS:
            blk = pos // s
            same = blk[None, :] == blk[:, None]
            pins.append(same & incl)
            sxs.append(same & later)
            lm.append((blk[:, None] % 2 == 1) & (blk[None, :] == blk[:, None] - 1))
        mats.append(np.concatenate(rows + pins + sxs, axis=0).astype(np.float32))
        masks.append(np.stack(lm).astype(np.float32))
    return np.stack(mats), np.stack(masks)


def _rwkv_consts():
    c = CHUNK
    out = []
    for rev in (False, True):
        pos, incl, strict = _order_consts(rev)
        b16 = pos // 16
        b32 = pos // 32
        same16 = b16[None, :] == b16[:, None]
        same32 = b32[None, :] == b32[:, None]
        out.append(np.stack([incl, strict, strict & same16, strict & same32 & ~same16,
                             strict & ~same32, np.eye(c, dtype=bool)]).astype(np.float32))
    return np.stack(out)


def _gla_kernel(uf_ref, ub_ref, s0_ref, cst_ref, msk_ref, hm_ref, wd_ref, bd_ref, of_ref, ob_ref, sf_ref, s_scr,
                *, heads, dk, dv):
    ci = pl.program_id(1)

    @pl.when(ci == 0)
    def _():
        s_scr[...] = s0_ref[0]

    c = CHUNK
    qk = heads * dk
    nl = len(GLA_LEVELS)
    hm = hm_ref[...]

    def rows_per_head(x):
        return jnp.concatenate([x] * heads, axis=0)

    for d, u_ref, o_ref in ((0, uf_ref, of_ref), (1, ub_ref, ob_ref)):
        u = u_ref[0]
        q = u[:, 0:qk] * (dk ** -0.5)
        k = u[:, qk:2 * qk]
        v = u[:, 2 * qk:2 * qk + heads * dv]
        z = u[:, 2 * qk + 2 * heads * dv:]
        gk = _log_sigmoid(_dot(z, wd_ref[d], precision=HI) + bd_ref[d]) * (1.0 / GLA_NORMALIZER)
        e = jnp.exp(_split_dot_l(cst_ref[d], gk))
        e_cum, e_rest = e[0:c], e[c:2 * c]
        qm = rows_per_head(q) * hm
        attn = _dot(qm.astype(BF16), k.astype(BF16), NT) * msk_ref[d, 0]
        for li in range(nl):
            qs = qm * rows_per_head(e[(2 + li) * c:(3 + li) * c])
            ks = k * e[(2 + nl + li) * c:(3 + nl + li) * c]
            attn = attn + _dot(qs.astype(BF16), ks.astype(BF16), NT) * msk_ref[d, 1 + li]
        st = s_scr[d]
        o_inter = _dot((qm * rows_per_head(e_cum)).astype(BF16), st.astype(BF16), NT)
        attn = attn.astype(BF16)
        vb = v.astype(BF16)
        outs = []
        for h in range(heads):
            outs.append(o_inter[h * c:(h + 1) * c] + _dot(attn[h * c:(h + 1) * c], vb[:, h * dv:(h + 1) * dv]))
        o_ref[0] = jnp.concatenate(outs, axis=-1)
        etot = jnp.exp(jnp.sum(gk, axis=0, keepdims=True))
        vstack = jnp.concatenate([vb[:, h * dv:(h + 1) * dv] for h in range(heads)], axis=0)
        s_scr[d] = st * etot + _dot(vstack, (rows_per_head(k * e_rest) * hm).astype(BF16), TN)

    @pl.when(ci == pl.num_programs(1) - 1)
    def _():
        sf_ref[0] = s_scr[...]


def gla(u_gla, s0c, w_dec, b_dec, heads, dk, dv):
    b, t, cols = u_gla.shape
    nc = t // CHUNK
    qk = heads * dk
    rank = w_dec.shape[1]
    cst, msk = _gla_consts()
    msk = np.tile(msk, (1, 1, heads, 1))
    hm = np.kron(np.eye(heads, dtype=np.float32), np.ones((CHUNK, dk), np.float32))
    wd = jnp.zeros((2, LANES, qk), F32)
    wd = wd.at[0, 0:rank].set(w_dec[0]).at[1, rank:2 * rank].set(w_dec[1])
    full = lambda a: pl.BlockSpec(a.shape, lambda i, j: (0,) * a.ndim)
    consts = [jnp.asarray(cst), jnp.asarray(msk), jnp.asarray(hm), wd, b_dec.reshape(2, 1, qk)]
    return pl.pallas_call(
        functools.partial(_gla_kernel, heads=heads, dk=dk, dv=dv),
        grid=(b, nc),
        in_specs=[pl.BlockSpec((1, CHUNK, cols), lambda i, j: (i, j, 0)),
                  pl.BlockSpec((1, CHUNK, cols), lambda i, j: (i, nc - 1 - j, 0)),
                  pl.BlockSpec((1, 2, dv, qk), lambda i, j: (i, 0, 0, 0))] + [full(a) for a in consts],
        out_specs=[pl.BlockSpec((1, CHUNK, heads * dv), lambda i, j: (i, j, 0)),
                   pl.BlockSpec((1, CHUNK, heads * dv), lambda i, j: (i, nc - 1 - j, 0)),
                   pl.BlockSpec((1, 2, dv, qk), lambda i, j: (i, 0, 0, 0))],
        out_shape=[jax.ShapeDtypeStruct((b, t, heads * dv), F32),
                   jax.ShapeDtypeStruct((b, t, heads * dv), F32),
                   jax.ShapeDtypeStruct((b, 2, dv, qk), F32)],
        scratch_shapes=[pltpu.VMEM((2, dv, qk), F32)],
        compiler_params=_cparams(("parallel", "arbitrary")),
    )(u_gla, u_gla, s0c, *consts)


def _gla_state_in(s):
    h, dk, dv = s.shape[-3:]
    return jnp.moveaxis(s, -1, -3).reshape(s.shape[:-3] + (dv, h * dk))


def _gla_state_out(s, heads):
    dv, qk = s.shape[-2:]
    return jnp.moveaxis(s.reshape(s.shape[:-2] + (dv, heads, qk // heads)), -3, -1)


def _dot3(x, w_ref):
    hi, lo = _two_terms(x)
    return _dot(hi, w_ref[0]) + _dot(lo, w_ref[0]) + _dot(hi, w_ref[1])


def _rwkv_kernel(uf_ref, ub_ref, s0_ref, msk_ref, w2_ref, w0_ref, a2_ref, a0_ref, g2_ref, kk_ref, ka_ref, rk_ref,
                 bones_ref, yf_ref, yb_ref, bonus_ref, gate_ref, sf_ref, s_scr, *, heads, n):
    ci = pl.program_id(1)
    g2h = 2 * heads

    @pl.when(ci == 0)
    def _():
        s_scr[...] = s0_ref[0].reshape(g2h, n, n)

    c = CHUNK
    wd = heads * n
    bones = bones_ref[...]

    def split_heads(x):
        return jnp.stack([x[:, h * n:(h + 1) * n] for h in range(heads)], axis=0)

    def masked(x, mi):
        return jnp.concatenate([jnp.where(msk_ref[d, mi] > 0, x[d * heads:(d + 1) * heads], 0.0) for d in (0, 1)],
                               axis=0)

    ars, kbs, vvs, kbhs, e_tots = [], [], [], [], []
    for d, u_ref in ((0, uf_ref), (1, ub_ref)):
        u = u_ref[0]
        r, k, v = u[:, 0:wd], u[:, wd:2 * wd], u[:, 2 * wd:3 * wd]
        z0 = 3 * wd
        zw = u[:, z0 + d * n:z0 + (d + 1) * n]
        za = u[:, z0 + 2 * n:z0 + 3 * n]
        wlog = -_softplus(-(w0_ref[d] + _dot3(jnp.tanh(zw), w2_ref.at[d]))) - 0.5
        logw = -jnp.exp(wlog)
        a = _sigmoid(a0_ref[...] + _dot3(za, a2_ref))
        kk = k * kk_ref[...]
        kk = kk * lax.rsqrt(jnp.maximum(_split_dot(kk * kk, bones), 1e-24))
        km = k * (1.0 + (a - 1.0) * ka_ref[...])
        bb = kk * a
        cum = _split_dot_l(msk_ref[d, 0], logw)
        tot = jnp.sum(logw, axis=0, keepdims=True)
        e_neg = jnp.exp(-cum)
        e_rest = jnp.exp(tot - cum)
        e_tots.append(jnp.exp(tot))
        ars.append(split_heads(jnp.concatenate([kk * jnp.exp(cum - logw), r * jnp.exp(cum)], axis=0)))
        kbs.append(split_heads(jnp.concatenate([km * e_neg, bb * e_neg], axis=0)))
        vvs.append(split_heads(v))
        kbhs.append(jnp.concatenate([km * e_rest, bb * e_rest], axis=0))
        if d == 0:
            bonus_ref[0] = _split_dot(r * km * rk_ref[...], bones) * v
            zg = u[:, z0 + 3 * n:z0 + 3 * n + g2_ref.shape[1]]
            gate_ref[0] = _dot3(_sigmoid(zg), g2_ref)

    ar = jnp.concatenate(ars, axis=0)
    kb = jnp.concatenate(kbs, axis=0)
    vv = jnp.concatenate(vvs, axis=0)
    st = s_scr[...]
    sc = _bmm(ar, kb, BNT)
    lak = masked(sc[:, :c, :c], 1)
    lab = masked(sc[:, :c, c:], 1)
    mrk = masked(sc[:, c:, :c], 0)
    mrb = masked(sc[:, c:, c:], 0)
    through = _bmm(ar, st, BNT)
    rhs = through[:, :c] + _bmm(lak, vv)
    ld = masked(lab, 2)
    x = msk_ref[0, 5] - ld
    p = _bmm(ld, ld)
    x = x + _bmm(x, p)
    p = _bmm(p, p)
    x = x + _bmm(x, p)
    p = _bmm(p, p)
    x = x + _bmm(x, p)
    for mi in (3, 4):
        x = x - _bmm(x, _bmm(masked(lab, mi), x))
    uu = _bmm(x, rhs)
    vu = jnp.concatenate([vv, -uu], axis=1)
    y = through[:, c:] + _bmm(jnp.concatenate([mrk, mrb], axis=2), vu)
    yf_ref[0] = jnp.concatenate([y[h] for h in range(heads)], axis=-1)
    yb_ref[0] = jnp.concatenate([y[heads + h] for h in range(heads)], axis=-1)
    for d in (0, 1):
        for h in range(heads):
            sl = slice(h * n, (h + 1) * n)
            g = d * heads + h
            s_scr[g] = st[g] * e_tots[d][:, sl] + _bdot(vu[g], kbhs[d][:, sl], TN)

    @pl.when(ci == pl.num_programs(1) - 1)
    def _():
        sf_ref[0] = s_scr[...].reshape(2, heads, n, n)


def rwkv(u, s0, w2, w0, a2, a0, g2, k_k, k_a, r_k, heads, n):
    b, t, cols = u.shape
    nc = t // CHUNK
    wd = heads * n
    msk = _rwkv_consts()
    bones = np.kron(np.eye(heads, dtype=np.float32), np.ones((n, n), np.float32))
    row = lambda x: x.reshape(1, wd)
    full = lambda a: pl.BlockSpec(a.shape, lambda i, j: (0,) * a.ndim)
    terms = lambda w: jnp.stack(_two_terms(w), axis=-3)
    args = [jnp.asarray(msk), terms(w2), w0.reshape(2, 1, wd), terms(a2), row(a0), terms(g2), row(k_k), row(k_a),
            row(r_k), jnp.asarray(bones)]
    return pl.pallas_call(
        functools.partial(_rwkv_kernel, heads=heads, n=n),
        grid=(b, nc),
        in_specs=[pl.BlockSpec((1, CHUNK, cols), lambda i, j: (i, j, 0)),
                  pl.BlockSpec((1, CHUNK, cols), lambda i, j: (i, nc - 1 - j, 0)),
                  pl.BlockSpec((1, 2, heads, n, n), lambda i, j: (i, 0, 0, 0, 0))] + [full(a) for a in args],
        out_specs=[pl.BlockSpec((1, CHUNK, wd), lambda i, j: (i, j, 0)),
                   pl.BlockSpec((1, CHUNK, wd), lambda i, j: (i, nc - 1 - j, 0)),
                   pl.BlockSpec((1, CHUNK, wd), lambda i, j: (i, j, 0)),
                   pl.BlockSpec((1, CHUNK, wd), lambda i, j: (i, j, 0)),
                   pl.BlockSpec((1, 2, heads, n, n), lambda i, j: (i, 0, 0, 0, 0))],
        out_shape=[jax.ShapeDtypeStruct((b, t, wd), F32)] * 4
        + [jax.ShapeDtypeStruct((b, 2, heads, n, n), F32)],
        scratch_shapes=[pltpu.VMEM((2 * heads, n, n), F32)],
        compiler_params=_cparams(("parallel", "arbitrary")),
    )(u, u, s0, *args)


def _proj_out_kernel(x_ref, mod_ref, of_ref, ob_ref, gg_ref, gn_ref, yf_ref, yb_ref, bonus_ref, gate_ref,
                     lng_ref, lnb_ref, bmean_ref, w_ref, o_ref, *, gla_heads):
    m = mod_ref[0]
    o = of_ref[0] + ob_ref[0]
    gw = o.shape[-1]
    dv = gw // gla_heads
    parts = []
    for h in range(gla_heads):
        oh = o[:, h * dv:(h + 1) * dv]
        parts.append(oh * lax.rsqrt(jnp.mean(oh * oh, axis=-1, keepdims=True) + EPS))
    o_gla = jnp.concatenate(parts, axis=-1) * gn_ref[...] * _silu(gg_ref[0])
    y = yf_ref[0] + yb_ref[0]
    bmean = bmean_ref[...]
    dy = y - _split_dot(y, bmean)
    var = _split_dot(dy * dy, bmean)
    yn = dy * lax.rsqrt(var + RWKV_LN_EPS) * lng_ref[...] + lnb_ref[...]
    o_rwkv = (yn + bonus_ref[0]) * gate_ref[0]
    w = w_ref[...]
    mix = _bdot(o_gla, w[:gw]) + _bdot(o_rwkv, w[gw:])
    o_ref[0] = x_ref[0] + m[2:3] * mix


def proj_out(x, mod, mod_row, o_f, o_b, u_gla, gla_norm, yf, yb, bonus, gate, ln_g, ln_b, w_out, gla_heads, rwkv_n,
             tm=256):
    b, t, d = x.shape
    gw = o_f.shape[-1]
    rw = yf.shape[-1]
    tm = min(tm, t)
    bmean = np.kron(np.eye(rw // rwkv_n, dtype=np.float32), np.full((rwkv_n, rwkv_n), 1.0 / rwkv_n, np.float32))
    tile = lambda wdt: pl.BlockSpec((1, tm, wdt), lambda i, j: (i, j, 0))
    row = lambda wdt: pl.BlockSpec((1, wdt), lambda i, j: (0, 0))
    g_col_block = (u_gla.shape[-1] - LANES - gw) // gw
    return pl.pallas_call(
        functools.partial(_proj_out_kernel, gla_heads=gla_heads),
        grid=(b, t // tm),
        in_specs=[tile(d),
                  pl.BlockSpec((1, 6, d), lambda i, j: (mod_row(i), 0, 0)),
                  tile(gw), tile(gw),
                  pl.BlockSpec((1, tm, gw), lambda i, j: (i, j, g_col_block)),
                  row(gw), tile(rw), tile(rw), tile(rw), tile(rw), row(rw), row(rw),
                  pl.BlockSpec((rw, rw), lambda i, j: (0, 0)),
                  pl.BlockSpec(w_out.shape, lambda i, j: (0, 0))],
        out_specs=tile(d),
        out_shape=jax.ShapeDtypeStruct((b, t, d), F32),
        compiler_params=_cparams(("parallel", "parallel")),
    )(x, mod, o_f, o_b, u_gla, gla_norm.reshape(1, gw), yf, yb, bonus, gate, ln_g.reshape(1, rw),
      ln_b.reshape(1, rw), jnp.asarray(bmean), w_out)


MOE_SUB = 256
MOE_CAP = 64
MOE_SUPER = 2048


def _route_kernel(x_ref, mod_ref, g_ref, rw_ref, rb_ref, tri_ref, h_ref, gates_ref, pos_ref, *, n_exp):
    m = mod_ref[0]
    h = _rms_mod(x_ref[...], g_ref[...], m[4:5], m[3:4])
    h_ref[...] = h.astype(BF16)
    tm = h.shape[0]
    gsz = n_exp // N_GROUPS
    scores = _sigmoid(_dot(rw_ref[...], h, NT, precision=HI))
    sel = scores + rb_ref[...]
    neg = jnp.float32(-jnp.inf)
    s3 = sel.reshape(N_GROUPS, gsz, tm)
    idx = lax.broadcasted_iota(jnp.int32, s3.shape, 1)
    m1 = jnp.max(s3, axis=1, keepdims=True)
    first = jnp.min(jnp.where(s3 == m1, idx, gsz), axis=1, keepdims=True)
    m2 = jnp.max(jnp.where(idx == first, neg, s3), axis=1, keepdims=True)
    gs = (m1 + m2).reshape(N_GROUPS, tm)
    gidx = lax.broadcasted_iota(jnp.int32, gs.shape, 0)
    cnt = jnp.zeros(gs.shape, jnp.int32)
    for g in range(N_GROUPS):
        other = gs[g:g + 1]
        cnt = cnt + ((other > gs) | ((other == gs) & (g < gidx))).astype(jnp.int32)
    gkeep = jnp.broadcast_to((cnt < TOPK_GROUPS)[:, None, :], s3.shape).reshape(n_exp, tm)
    selm = jnp.where(gkeep, sel, neg)
    eidx = lax.broadcasted_iota(jnp.int32, selm.shape, 0)
    rank = jnp.zeros(selm.shape, jnp.int32)
    for e in range(n_exp):
        other = selm[e:e + 1]
        rank = rank + ((other > selm) | ((other == selm) & (e < eidx))).astype(jnp.int32)
    chosen = rank < TOP_K
    wts = jnp.where(chosen, scores, 0.0)
    gates_ref[...] = wts / jnp.sum(wts, axis=0, keepdims=True) * ROUTED_SCALE
    pos_ref[...] = _dot(chosen.astype(BF16), tri_ref[...])


def route(x2, mod, mod_row, norm_g, router_w, router_b, tm):
    nt, d = x2.shape
    n_exp = router_w.shape[1]
    tri = np.triu(np.ones((tm, tm), np.float32), 1)
    return pl.pallas_call(
        functools.partial(_route_kernel, n_exp=n_exp),
        grid=(nt // tm,),
        in_specs=[pl.BlockSpec((tm, d), lambda i: (i, 0)),
                  pl.BlockSpec((1, 6, d), lambda i: (mod_row(i, tm), 0, 0)),
                  pl.BlockSpec((1, d), lambda i: (0, 0)),
                  pl.BlockSpec((n_exp, d), lambda i: (0, 0)),
                  pl.BlockSpec((n_exp, 1), lambda i: (0, 0)),
                  pl.BlockSpec((tm, tm), lambda i: (0, 0))],
        out_specs=[pl.BlockSpec((tm, d), lambda i: (i, 0)),
                   pl.BlockSpec((n_exp, tm), lambda i: (0, i)),
                   pl.BlockSpec((n_exp, tm), lambda i: (0, i))],
        out_shape=[jax.ShapeDtypeStruct((nt, d), BF16),
                   jax.ShapeDtypeStruct((n_exp, nt), F32),
                   jax.ShapeDtypeStruct((n_exp, nt), F32)],
        compiler_params=_cparams(("parallel",)),
    )(x2, mod, norm_g.reshape(1, d), router_w.T, router_b.reshape(n_exp, 1), jnp.asarray(tri, BF16))


def _swiglu(h, w1, w3, w2):
    return _bdot(_silu(_bdot(h, w1)) * _bdot(h, w3), w2)


def _moe_kernel(npass_ref, x_ref, mod_ref, h_ref, gates_ref, pos_ref, w1_ref, w3_ref, w2_ref, s1_ref, s3_ref, s2_ref,
                o_ref, xe_ref, *, sub, cap):
    i = pl.program_id(0)
    e = pl.program_id(1)
    n_exp = pl.num_programs(1)
    n_sub = h_ref.shape[0] // sub

    @pl.when(e == 0)
    def _():
        o_ref[...] = _swiglu(h_ref[...], s1_ref[...], s3_ref[...], s2_ref[...])

    gate = gates_ref[pl.ds(e, 1), :]
    pos = pos_ref[pl.ds(e, 1), :]
    rows = lax.broadcasted_iota(jnp.int32, (cap, sub), 0).astype(F32)

    def one_pass(it, carry):
        base = (it * cap).astype(F32)
        picks = []
        for s in range(n_sub):
            sl = slice(s * sub, (s + 1) * sub)
            g = gate[:, sl]
            pick = jnp.where((pos[:, sl] - base == rows) & (g > 0.0), g, 0.0)
            picks.append(pick.astype(BF16))
            onehot = (pick > 0.0).astype(BF16)
            xe_ref[s * cap:(s + 1) * cap, :] = _dot(onehot, h_ref[sl, :]).astype(BF16)
        ye = _swiglu(xe_ref[...], w1_ref[0], w3_ref[0], w2_ref[0]).astype(BF16)
        for s in range(n_sub):
            sl = slice(s * sub, (s + 1) * sub)
            o_ref[sl, :] += _dot(picks[s], ye[s * cap:(s + 1) * cap], TN)
        return carry

    lax.fori_loop(0, npass_ref[i * n_exp + e], one_pass, 0)

    @pl.when(e == n_exp - 1)
    def _():
        o_ref[...] = x_ref[...] + mod_ref[0][5:6] * o_ref[...]


def moe(x2, mod, mod_row, h, gates, pos, w1, w3, w2, s1, s3, s2, sub, tm):
    nt, d = x2.shape
    n_exp, _, ff = w1.shape
    cap = min(MOE_CAP, sub)
    n_sub = tm // sub
    cnt = (pos + (gates > 0.0))[:, sub - 1::sub].reshape(n_exp, nt // tm, n_sub).max(axis=-1)
    npass = ((cnt.astype(jnp.int32) + cap - 1) // cap).T.reshape(-1)
    full = lambda a: pl.BlockSpec(a.shape, lambda i, e, n: (0,) * a.ndim)
    grid_spec = pltpu.PrefetchScalarGridSpec(
        num_scalar_prefetch=1,
        grid=(nt // tm, n_exp),
        in_specs=[pl.BlockSpec((tm, d), lambda i, e, n: (i, 0)),
                  pl.BlockSpec((1, 6, d), lambda i, e, n: (mod_row(i, tm), 0, 0)),
                  pl.BlockSpec((tm, d), lambda i, e, n: (i, 0)),
                  pl.BlockSpec((n_exp, tm), lambda i, e, n: (0, i)),
                  pl.BlockSpec((n_exp, tm), lambda i, e, n: (0, i)),
                  pl.BlockSpec((1, d, ff), lambda i, e, n: (e, 0, 0)),
                  pl.BlockSpec((1, d, ff), lambda i, e, n: (e, 0, 0)),
                  pl.BlockSpec((1, ff, d), lambda i, e, n: (e, 0, 0)),
                  full(s1), full(s3), full(s2)],
        out_specs=pl.BlockSpec((tm, d), lambda i, e, n: (i, 0)),
        scratch_shapes=[pltpu.VMEM((n_sub * cap, d), BF16)])
    return pl.pallas_call(
        functools.partial(_moe_kernel, sub=sub, cap=cap),
        grid_spec=grid_spec,
        out_shape=jax.ShapeDtypeStruct((nt, d), F32),
        compiler_params=_cparams(("parallel", "arbitrary")),
    )(npass, x2, mod, h, gates, pos, w1, w3, w2, s1, s3, s2)


def _final_kernel(x_ref, g_ref, o_ref):
    x = x_ref[...]
    o_ref[...] = x * lax.rsqrt(jnp.mean(x * x, axis=-1, keepdims=True) + EPS) * g_ref[...]


def final_norm(x2, g, tm=512):
    nt, d = x2.shape
    tm = min(tm, nt)
    return pl.pallas_call(
        _final_kernel,
        grid=(nt // tm,),
        in_specs=[pl.BlockSpec((tm, d), lambda i: (i, 0)), pl.BlockSpec((1, d), lambda i: (0, 0))],
        out_specs=pl.BlockSpec((tm, d), lambda i: (i, 0)),
        out_shape=jax.ShapeDtypeStruct((nt, d), F32),
        compiler_params=_cparams(("parallel",)),
    )(x2, g.reshape(1, d))


def _pad_cols(a, n):
    return jnp.pad(a, [(0, 0)] * (a.ndim - 1) + [(0, n - a.shape[-1])])


def _round_up(n, m):
    return -(-n // m) * m


def _layer(x, mod, latent, s_gla_c, s_rwkv, p, dims, grid_w):
    b, t, d = x.shape
    ctx_row = mod.shape[0] - 1
    brow = (lambda i: i) if latent else (lambda i: ctx_row)
    trow = (lambda i, tm: (i * tm) // t) if latent else (lambda i, tm: ctx_row)
    u_gla, u_rwkv = proj_in(x, mod, brow, p['norm1'], p['w_in'], dims['gla_cols'])
    hh, ww = (t // grid_w, grid_w) if latent else (b, t)
    u_rwkv = conv(u_rwkv, p['rwkv_conv'], hh, ww, vertical=latent)
    o_f, o_b, sg = gla(u_gla, s_gla_c, p['gla_w_dec'], p['gla_b_dec'], dims['gla_heads'], dims['gla_dk'], dims['gla_dv'])
    yf, yb, bonus, gate, sr = rwkv(u_rwkv, s_rwkv, p['rwkv_w2'], p['rwkv_w0'], p['rwkv_a2'], p['rwkv_a0'],
                                   p['rwkv_g2'], p['rwkv_k_k'], p['rwkv_k_a'], p['rwkv_r_k'],
                                   dims['rwkv_heads'], dims['rwkv_n'])
    x = proj_out(x, mod, brow, o_f, o_b, u_gla, p['gla_norm'], yf, yb, bonus, gate, p['rwkv_ln_g'], p['rwkv_ln_b'],
                 p['w_out'], dims['gla_heads'], dims['rwkv_n'])
    x2 = x.reshape(b * t, d)
    sub = min(MOE_SUB, t if latent else b * t)
    tm = min(MOE_SUPER, t if latent else b * t)
    h, gates, pos = route(x2, mod, trow, p['norm2'], p['router_w'], p['router_b'], sub)
    x2 = moe(x2, mod, trow, h, gates, pos, p['exp_w1'], p['exp_w3'], p['exp_w2'], p['sh_w1'], p['sh_w3'], p['sh_w2'],
             sub, tm)
    return x2.reshape(b, t, d), sg, sr


def kernel(x_prompt, x_sample, c, state_gla, state_rwkv, c_ctx, ada_w, ada_b, norm1, norm2, norm_f, w_in, w_out,
           gla_w_dec, gla_b_dec, gla_norm, rwkv_conv, rwkv_w2, rwkv_w0, rwkv_a2, rwkv_a0, rwkv_g2, rwkv_k_k, rwkv_k_a,
           rwkv_r_k, rwkv_ln_g, rwkv_ln_b, router_w, router_b, exp_w1, exp_w3, exp_w2, sh_w1, sh_w3, sh_w2):
    depth, d, _ = ada_w.shape
    bp = x_prompt.shape[0]
    bs = x_sample.shape[0]
    grid_w = 64
    gla_heads, gla_dk, gla_dv = state_gla.shape[3:]
    rwkv_heads, rwkv_n = state_rwkv.shape[3:5]
    gla_qk, gla_w, rwkv_w = gla_heads * gla_dk, gla_heads * gla_dv, rwkv_heads * rwkv_n
    gla_cols_raw = 2 * gla_qk + 2 * gla_w + 2 * gla_w_dec.shape[2]
    rwkv_cols_raw = w_in.shape[2] - gla_cols_raw
    gla_cols = 2 * gla_qk + 2 * gla_w + LANES
    rwkv_cols = _round_up(rwkv_cols_raw, LANES)
    dims = dict(gla_cols=gla_cols, gla_heads=gla_heads, gla_dk=gla_dk, gla_dv=gla_dv,
                rwkv_heads=rwkv_heads, rwkv_n=rwkv_n)

    w_in_p = jnp.concatenate([_pad_cols(w_in[..., :gla_cols_raw], gla_cols),
                              _pad_cols(w_in[..., gla_cols_raw:], rwkv_cols)], axis=-1).astype(BF16)
    conv_p = _pad_cols(rwkv_conv, rwkv_cols)
    w_out_b = w_out.astype(BF16)
    e1, e3, e2 = exp_w1.astype(BF16), exp_w3.astype(BF16), exp_w2.astype(BF16)
    s1, s3, s2 = sh_w1.astype(BF16), sh_w3.astype(BF16), sh_w2.astype(BF16)

    rows = _round_up(bs + 1, 8)
    cvec = jnp.concatenate([c, c_ctx[None], jnp.zeros((rows - bs - 1, d), F32)], axis=0)
    mod_all = adaln(cvec, ada_w, ada_b).reshape(depth, rows, 6, d)[:, :bs + 1]

    zero_gla = jnp.zeros((bp, 2, gla_dv, gla_qk), F32)
    zero_rwkv = jnp.zeros((bp, 2, rwkv_heads, rwkv_n, rwkv_n), F32)
    state_gla_c = _gla_state_in(state_gla)

    xp, xs = x_prompt, x_sample
    new_gla, new_rwkv = [], []
    for l in range(depth):
        p = dict(norm1=norm1[l], norm2=norm2[l], w_in=w_in_p[l], w_out=w_out_b[l], gla_w_dec=gla_w_dec[l],
                 gla_b_dec=gla_b_dec[l], gla_norm=gla_norm[l], rwkv_conv=conv_p[l], rwkv_w2=rwkv_w2[l],
                 rwkv_w0=rwkv_w0[l], rwkv_a2=rwkv_a2[l], rwkv_a0=rwkv_a0[l], rwkv_g2=rwkv_g2[l],
                 rwkv_k_k=rwkv_k_k[l], rwkv_k_a=rwkv_k_a[l], rwkv_r_k=rwkv_r_k[l], rwkv_ln_g=rwkv_ln_g[l],
                 rwkv_ln_b=rwkv_ln_b[l], router_w=router_w[l], router_b=router_b[l], exp_w1=e1[l], exp_w3=e3[l],
                 exp_w2=e2[l], sh_w1=s1[l], sh_w3=s3[l], sh_w2=s2[l])
        mod = mod_all[l]
        xp, sg, sr = _layer(xp, mod, False, zero_gla, zero_rwkv, p, dims, grid_w)
        new_gla.append(_gla_state_out(sg, gla_heads))
        new_rwkv.append(sr)
        xs, _, _ = _layer(xs, mod, True, state_gla_c[:, l], state_rwkv[:, l], p, dims, grid_w)
    y_prompt = final_norm(xp.reshape(-1, d), norm_f).reshape(xp.shape)
    y_sample = final_norm(xs.reshape(-1, d), norm_f).reshape(xs.shape)
    return (y_prompt, y_sample, jnp.stack(new_gla, axis=1), jnp.stack(new_rwkv, axis=1))
```

```python
import functools

import numpy as np
import jax
import jax.numpy as jnp
from jax import lax
from jax.experimental import pallas as pl
from jax.experimental.pallas import tpu as pltpu

F32 = jnp.float32
BF16 = jnp.bfloat16
HI = lax.Precision.HIGHEST

LANES = 128
EPS = 1e-6
RWKV_LN_EPS = 64e-5
GLA_NORMALIZER = 16.0
ROUTED_SCALE = 2.5
N_GROUPS = 8
TOPK_GROUPS = 4
TOP_K = 8
CHUNK = 64
VMEM_LIMIT = 56 * 1024 * 1024


def _cparams(sem):
    return pltpu.CompilerParams(dimension_semantics=sem, vmem_limit_bytes=VMEM_LIMIT)


def _dot(a, b, dims=(((1,), (0,)), ((), ())), precision=None):
    return lax.dot_general(a, b, dims, precision=precision, preferred_element_type=F32)


def _bdot(a, b, dims=(((1,), (0,)), ((), ()))):
    return lax.dot_general(a.astype(BF16), b.astype(BF16), dims, preferred_element_type=F32)


NT = (((1,), (1,)), ((), ()))
TN = (((0,), (0,)), ((), ()))
BNN = (((2,), (1,)), ((0,), (0,)))
BNT = (((2,), (2,)), ((0,), (0,)))


def _bmm(a, b, dims=BNN):
    return lax.dot_general(a.astype(BF16), b.astype(BF16), dims, preferred_element_type=F32)


def _two_terms(x):
    hi = x.astype(BF16)
    return hi, (x - hi.astype(F32)).astype(BF16)


def _split_dot(x, m):
    hi, lo = _two_terms(x)
    mb = m.astype(BF16)
    return _dot(hi, mb) + _dot(lo, mb)


def _split_dot_l(m, x):
    hi, lo = _two_terms(x)
    mb = m.astype(BF16)
    return _dot(mb, hi) + _dot(mb, lo)


def _sigmoid(x):
    return 1.0 / (1.0 + jnp.exp(-x))


def _silu(x):
    return x * _sigmoid(x)


def _softplus(x):
    return jnp.maximum(x, 0.0) + jnp.log(1.0 + jnp.exp(-jnp.abs(x)))


def _log_sigmoid(x):
    return -_softplus(-x)


def _rms_mod(x, g, scale, shift):
    y = x * lax.rsqrt(jnp.mean(x * x, axis=-1, keepdims=True) + EPS)
    return (y * g) * (1.0 + scale) + shift


def _adaln_kernel(c_ref, w_ref, b_ref, o_ref):
    c = c_ref[...]
    o_ref[0] = _dot(_silu(c), w_ref[0], precision=HI) + b_ref[0]


def adaln(cvec, ada_w, ada_b, tn=512):
    depth, d, n = ada_w.shape
    r = cvec.shape[0]
    return pl.pallas_call(
        _adaln_kernel,
        grid=(depth, n // tn),
        in_specs=[pl.BlockSpec((r, d), lambda l, j: (0, 0)),
                  pl.BlockSpec((1, d, tn), lambda l, j: (l, 0, j)),
                  pl.BlockSpec((1, 1, tn), lambda l, j: (l, 0, j))],
        out_specs=pl.BlockSpec((1, r, tn), lambda l, j: (l, 0, j)),
        out_shape=jax.ShapeDtypeStruct((depth, r, n), F32),
        compiler_params=_cparams(("parallel", "parallel")),
    )(cvec, ada_w, ada_b.reshape(depth, 1, n))


def _proj_in_kernel(x_ref, mod_ref, g_ref, w_ref, og_ref, or_ref, *, n_gla):
    m = mod_ref[0]
    h = _rms_mod(x_ref[0], g_ref[...], m[1:2], m[0:1])
    u = _bdot(h, w_ref[...])
    og_ref[0] = u[:, :n_gla]
    or_ref[0] = u[:, n_gla:]


def proj_in(x, mod, mod_row, norm_g, w, n_gla, tm=256):
    b, t, d = x.shape
    n = w.shape[1]
    tm = min(tm, t)
    return pl.pallas_call(
        functools.partial(_proj_in_kernel, n_gla=n_gla),
        grid=(b, t // tm),
        in_specs=[pl.BlockSpec((1, tm, d), lambda i, j: (i, j, 0)),
                  pl.BlockSpec((1, 6, d), lambda i, j: (mod_row(i), 0, 0)),
                  pl.BlockSpec((1, d), lambda i, j: (0, 0)),
                  pl.BlockSpec((d, n), lambda i, j: (0, 0))],
        out_specs=[pl.BlockSpec((1, tm, n_gla), lambda i, j: (i, j, 0)),
                   pl.BlockSpec((1, tm, n - n_gla), lambda i, j: (i, j, 0))],
        out_shape=[jax.ShapeDtypeStruct((b, t, n_gla), F32),
                   jax.ShapeDtypeStruct((b, t, n - n_gla), F32)],
        compiler_params=_cparams(("parallel", "parallel")),
    )(x, mod, norm_g.reshape(1, d), w)


CONV_PAD = 8


def _conv_kernel(u_ref, w_ref, o_ref, pad_ref, *, hh, ww, vertical):
    pad_ref[...] = jnp.zeros(pad_ref.shape, F32)
    pad_ref[1:hh + 1, CONV_PAD:CONV_PAD + ww, :] = u_ref[0]
    w = w_ref[...]
    acc = jnp.zeros((hh, ww, u_ref.shape[-1]), F32)
    for dy in range(3):
        if not vertical and dy != 1:
            continue
        for dx in range(3):
            acc = acc + pad_ref[dy:dy + hh, CONV_PAD - 1 + dx:CONV_PAD - 1 + dx + ww, :] * w[dy, dx]
    o_ref[0] = acc


def conv(u, wconv, hh, ww, vertical):
    c = u.shape[-1]
    b = u.size // (hh * ww * c)
    out = pl.pallas_call(
        functools.partial(_conv_kernel, hh=hh, ww=ww, vertical=vertical),
        grid=(b, c // LANES),
        in_specs=[pl.BlockSpec((1, hh, ww, LANES), lambda i, j: (i, 0, 0, j)),
                  pl.BlockSpec((3, 3, LANES), lambda i, j: (0, 0, j))],
        out_specs=pl.BlockSpec((1, hh, ww, LANES), lambda i, j: (i, 0, 0, j)),
        out_shape=jax.ShapeDtypeStruct((b, hh, ww, c), F32),
        scratch_shapes=[pltpu.VMEM((hh + 2, ww + 2 * CONV_PAD, LANES), F32)],
        compiler_params=_cparams(("parallel", "parallel")),
    )(u.reshape(b, hh, ww, c), wconv)
    return out.reshape(u.shape)


GLA_LEVELS = (32, 16, 8, 4, 2)


def _order_consts(rev):
    c = CHUNK
    idx = np.arange(c)
    pos = (c - 1 - idx) if rev else idx
    incl = (pos[None, :] <= pos[:, None])
    strict = (pos[None, :] < pos[:, None])
    return pos, incl, strict


def _gla_consts():
    c = CHUNK
    mats, masks = [], []
    for rev in (False, True):
        pos, incl, strict = _order_consts(rev)
        later = (pos[None, :] > pos[:, None])
        rows = [incl, later]
        pins, sxs, lm = [], [], [np.eye(c, dtype=bool)]
        for s in (1,) + GLA_LEVELS:
            blk = pos // s
            same = blk[None, :] == blk[:, None]
            if s > 1:
                pins.append(same & incl)
                sxs.append(same & later)
            lm.append(((blk[:, None] % 2 == 1) & (blk[None, :] == blk[:, None] - 1)).T)
        mats.append(np.concatenate(rows + pins + sxs, axis=0).astype(np.float32))
        masks.append(np.stack(lm).astype(np.float32))
    return np.stack(mats), np.stack(masks)


def _rwkv_consts():
    c = CHUNK
    out = []
    for rev in (False, True):
        pos, incl, strict = _order_consts(rev)
        b16 = pos // 16
        b32 = pos // 32
        same16 = b16[None, :] == b16[:, None]
        same32 = b32[None, :] == b32[:, None]
        out.append(np.stack([incl, strict, strict & same16, strict & same32 & ~same16,
                             strict & ~same32, np.eye(c, dtype=bool)]).astype(np.float32))
    return np.stack(out)


def _gla_kernel(uf_ref, ub_ref, s0_ref, cst_ref, msk_ref, hm_ref, wd_ref, bd_ref, of_ref, ob_ref, sf_ref, s_scr,
                *, heads, dk, dv):
    ci = pl.program_id(1)

    @pl.when(ci == 0)
    def _():
        s_scr[...] = s0_ref[0]

    c = CHUNK
    qk = heads * dk
    nl = len(GLA_LEVELS)
    hm = hm_ref[...]

    def rows_per_head(x):
        return jnp.concatenate([x] * heads, axis=0)

    for d, u_ref, o_ref in ((0, uf_ref, of_ref), (1, ub_ref, ob_ref)):
        u = u_ref[0]
        q = u[:, 0:qk] * (dk ** -0.5)
        k = u[:, qk:2 * qk]
        v = u[:, 2 * qk:2 * qk + heads * dv]
        z = u[:, 2 * qk + 2 * heads * dv:]
        gk = _log_sigmoid(_dot(z, wd_ref[d], precision=HI) + bd_ref[d]) * (1.0 / GLA_NORMALIZER)
        e = jnp.exp(_split_dot_l(cst_ref[d], gk))
        e_cum, e_rest = e[0:c], e[c:2 * c]
        qm = rows_per_head(q) * hm
        kb = k.astype(BF16)
        attn = _dot(kb, qm.astype(BF16), NT) * msk_ref[d, 0]
        attn = attn + _dot(kb, (qm * rows_per_head(jnp.exp(gk))).astype(BF16), NT) * msk_ref[d, 1]
        for li in range(nl):
            qs = qm * rows_per_head(e[(2 + li) * c:(3 + li) * c])
            ks = k * e[(2 + nl + li) * c:(3 + nl + li) * c]
            attn = attn + _dot(ks.astype(BF16), qs.astype(BF16), NT) * msk_ref[d, 2 + li]
        st = s_scr[d]
        o_inter = _dot((qm * rows_per_head(e_cum)).astype(BF16), st.astype(BF16), NT)
        vb = v.astype(BF16)
        o_intra = _dot(attn.astype(BF16), vb, TN)
        o_ref[0] = jnp.concatenate([o_inter[h * c:(h + 1) * c] + o_intra[h * c:(h + 1) * c, h * dv:(h + 1) * dv]
                                    for h in range(heads)], axis=-1)
        etot = jnp.exp(jnp.sum(gk, axis=0, keepdims=True))
        vstack = jnp.concatenate([vb[:, h * dv:(h + 1) * dv] for h in range(heads)], axis=0)
        s_scr[d] = st * etot + _dot(vstack, (rows_per_head(k * e_rest) * hm).astype(BF16), TN)

    @pl.when(ci == pl.num_programs(1) - 1)
    def _():
        sf_ref[0] = s_scr[...]


def gla(u_gla, s0c, w_dec, b_dec, heads, dk, dv):
    b, t, cols = u_gla.shape
    nc = t // CHUNK
    qk = heads * dk
    rank = w_dec.shape[1]
    cst, msk = _gla_consts()
    msk = np.tile(msk, (1, 1, 1, heads))
    hm = np.kron(np.eye(heads, dtype=np.float32), np.ones((CHUNK, dk), np.float32))
    wd = jnp.zeros((2, LANES, qk), F32)
    wd = wd.at[0, 0:rank].set(w_dec[0]).at[1, rank:2 * rank].set(w_dec[1])
    full = lambda a: pl.BlockSpec(a.shape, lambda i, j: (0,) * a.ndim)
    consts = [jnp.asarray(cst), jnp.asarray(msk), jnp.asarray(hm), wd, b_dec.reshape(2, 1, qk)]
    return pl.pallas_call(
        functools.partial(_gla_kernel, heads=heads, dk=dk, dv=dv),
        grid=(b, nc),
        in_specs=[pl.BlockSpec((1, CHUNK, cols), lambda i, j: (i, j, 0)),
                  pl.BlockSpec((1, CHUNK, cols), lambda i, j: (i, nc - 1 - j, 0)),
                  pl.BlockSpec((1, 2, dv, qk), lambda i, j: (i, 0, 0, 0))] + [full(a) for a in consts],
        out_specs=[pl.BlockSpec((1, CHUNK, heads * dv), lambda i, j: (i, j, 0)),
                   pl.BlockSpec((1, CHUNK, heads * dv), lambda i, j: (i, nc - 1 - j, 0)),
                   pl.BlockSpec((1, 2, dv, qk), lambda i, j: (i, 0, 0, 0))],
        out_shape=[jax.ShapeDtypeStruct((b, t, heads * dv), F32),
                   jax.ShapeDtypeStruct((b, t, heads * dv), F32),
                   jax.ShapeDtypeStruct((b, 2, dv, qk), F32)],
        scratch_shapes=[pltpu.VMEM((2, dv, qk), F32)],
        compiler_params=_cparams(("parallel", "arbitrary")),
    )(u_gla, u_gla, s0c, *consts)


def _gla_state_in(s):
    h, dk, dv = s.shape[-3:]
    return jnp.moveaxis(s, -1, -3).reshape(s.shape[:-3] + (dv, h * dk))


def _gla_state_out(s, heads):
    dv, qk = s.shape[-2:]
    return jnp.moveaxis(s.reshape(s.shape[:-2] + (dv, heads, qk // heads)), -3, -1)


def _dot3(x, w_ref):
    hi, lo = _two_terms(x)
    return _dot(hi, w_ref[0]) + _dot(lo, w_ref[0]) + _dot(hi, w_ref[1])


def _rwkv_kernel(uf_ref, ub_ref, s0_ref, msk_ref, w2_ref, w0_ref, a2_ref, a0_ref, g2_ref, kk_ref, ka_ref, rk_ref,
                 bones_ref, yf_ref, yb_ref, bonus_ref, gate_ref, sf_ref, s_scr, *, heads, n):
    ci = pl.program_id(1)
    g2h = 2 * heads

    @pl.when(ci == 0)
    def _():
        s_scr[...] = s0_ref[0].reshape(g2h, n, n)

    c = CHUNK
    wd = heads * n
    bones = bones_ref[...]

    def split_heads(x):
        return jnp.stack([x[:, h * n:(h + 1) * n] for h in range(heads)], axis=0)

    def masked(x, mi):
        return jnp.concatenate([jnp.where(msk_ref[d, mi] > 0, x[d * heads:(d + 1) * heads], 0.0) for d in (0, 1)],
                               axis=0)

    ars, kbs, vvs, kbhs, e_tots = [], [], [], [], []
    for d, u_ref in ((0, uf_ref), (1, ub_ref)):
        u = u_ref[0]
        r, k, v = u[:, 0:wd], u[:, wd:2 * wd], u[:, 2 * wd:3 * wd]
        z0 = 3 * wd
        zw = u[:, z0 + d * n:z0 + (d + 1) * n]
        za = u[:, z0 + 2 * n:z0 + 3 * n]
        wlog = -_softplus(-(w0_ref[d] + _dot3(jnp.tanh(zw), w2_ref.at[d]))) - 0.5
        logw = -jnp.exp(wlog)
        a = _sigmoid(a0_ref[...] + _dot3(za, a2_ref))
        kk = k * kk_ref[...]
        kk = kk * lax.rsqrt(jnp.maximum(_split_dot(kk * kk, bones), 1e-24))
        km = k * (1.0 + (a - 1.0) * ka_ref[...])
        bb = kk * a
        cum = _split_dot_l(msk_ref[d, 0], logw)
        tot = jnp.sum(logw, axis=0, keepdims=True)
        e_neg = jnp.exp(-cum)
        e_rest = jnp.exp(tot - cum)
        e_tots.append(jnp.exp(tot))
        ars.append(split_heads(jnp.concatenate([kk * jnp.exp(cum - logw), r * jnp.exp(cum)], axis=0)))
        kbs.append(split_heads(jnp.concatenate([km * e_neg, bb * e_neg], axis=0)))
        vvs.append(split_heads(v))
        kbhs.append(jnp.concatenate([km * e_rest, bb * e_rest], axis=0))
        if d == 0:
            bonus_ref[0] = _split_dot(r * km * rk_ref[...], bones) * v
            zg = u[:, z0 + 3 * n:z0 + 3 * n + g2_ref.shape[1]]
            gate_ref[0] = _dot3(_sigmoid(zg), g2_ref)

    ar = jnp.concatenate(ars, axis=0)
    kb = jnp.concatenate(kbs, axis=0)
    vv = jnp.concatenate(vvs, axis=0)
    st = s_scr[...]
    sc = _bmm(ar, kb, BNT)
    lak = masked(sc[:, :c, :c], 1)
    lab = masked(sc[:, :c, c:], 1)
    mrk = masked(sc[:, c:, :c], 0)
    mrb = masked(sc[:, c:, c:], 0)
    through = _bmm(ar, st, BNT)
    rhs = through[:, :c] + _bmm(lak, vv)
    ld = masked(lab, 2)
    x = msk_ref[0, 5] - ld
    p = _bmm(ld, ld)
    x = x + _bmm(x, p)
    p = _bmm(p, p)
    x = x + _bmm(x, p)
    p = _bmm(p, p)
    x = x + _bmm(x, p)
    for mi in (3, 4):
        x = x - _bmm(x, _bmm(masked(lab, mi), x))
    uu = _bmm(x, rhs)
    vu = jnp.concatenate([vv, -uu], axis=1)
    y = through[:, c:] + _bmm(jnp.concatenate([mrk, mrb], axis=2), vu)
    yf_ref[0] = jnp.concatenate([y[h] for h in range(heads)], axis=-1)
    yb_ref[0] = jnp.concatenate([y[heads + h] for h in range(heads)], axis=-1)
    for d in (0, 1):
        for h in range(heads):
            sl = slice(h * n, (h + 1) * n)
            g = d * heads + h
            s_scr[g] = st[g] * e_tots[d][:, sl] + _bdot(vu[g], kbhs[d][:, sl], TN)

    @pl.when(ci == pl.num_programs(1) - 1)
    def _():
        sf_ref[0] = s_scr[...].reshape(2, heads, n, n)


def rwkv(u, s0, w2, w0, a2, a0, g2, k_k, k_a, r_k, heads, n):
    b, t, cols = u.shape
    nc = t // CHUNK
    wd = heads * n
    msk = _rwkv_consts()
    bones = np.kron(np.eye(heads, dtype=np.float32), np.ones((n, n), np.float32))
    row = lambda x: x.reshape(1, wd)
    full = lambda a: pl.BlockSpec(a.shape, lambda i, j: (0,) * a.ndim)
    terms = lambda w: jnp.stack(_two_terms(w), axis=-3)
    args = [jnp.asarray(msk), terms(w2), w0.reshape(2, 1, wd), terms(a2), row(a0), terms(g2), row(k_k), row(k_a),
            row(r_k), jnp.asarray(bones)]
    return pl.pallas_call(
        functools.partial(_rwkv_kernel, heads=heads, n=n),
        grid=(b, nc),
        in_specs=[pl.BlockSpec((1, CHUNK, cols), lambda i, j: (i, j, 0)),
                  pl.BlockSpec((1, CHUNK, cols), lambda i, j: (i, nc - 1 - j, 0)),
                  pl.BlockSpec((1, 2, heads, n, n), lambda i, j: (i, 0, 0, 0, 0))] + [full(a) for a in args],
        out_specs=[pl.BlockSpec((1, CHUNK, wd), lambda i, j: (i, j, 0)),
                   pl.BlockSpec((1, CHUNK, wd), lambda i, j: (i, nc - 1 - j, 0)),
                   pl.BlockSpec((1, CHUNK, wd), lambda i, j: (i, j, 0)),
                   pl.BlockSpec((1, CHUNK, wd), lambda i, j: (i, j, 0)),
                   pl.BlockSpec((1, 2, heads, n, n), lambda i, j: (i, 0, 0, 0, 0))],
        out_shape=[jax.ShapeDtypeStruct((b, t, wd), F32)] * 4
        + [jax.ShapeDtypeStruct((b, 2, heads, n, n), F32)],
        scratch_shapes=[pltpu.VMEM((2 * heads, n, n), F32)],
        compiler_params=_cparams(("parallel", "arbitrary")),
    )(u, u, s0, *args)


def _proj_out_kernel(x_ref, mod_ref, of_ref, ob_ref, gg_ref, gn_ref, yf_ref, yb_ref, bonus_ref, gate_ref,
                     lng_ref, lnb_ref, bmean_ref, w_ref, o_ref, *, gla_heads):
    m = mod_ref[0]
    o = of_ref[0] + ob_ref[0]
    gw = o.shape[-1]
    dv = gw // gla_heads
    parts = []
    for h in range(gla_heads):
        oh = o[:, h * dv:(h + 1) * dv]
        parts.append(oh * lax.rsqrt(jnp.mean(oh * oh, axis=-1, keepdims=True) + EPS))
    o_gla = jnp.concatenate(parts, axis=-1) * gn_ref[...] * _silu(gg_ref[0])
    y = yf_ref[0] + yb_ref[0]
    bmean = bmean_ref[...]
    dy = y - _split_dot(y, bmean)
    var = _split_dot(dy * dy, bmean)
    yn = dy * lax.rsqrt(var + RWKV_LN_EPS) * lng_ref[...] + lnb_ref[...]
    o_rwkv = (yn + bonus_ref[0]) * gate_ref[0]
    w = w_ref[...]
    mix = _bdot(o_gla, w[:gw]) + _bdot(o_rwkv, w[gw:])
    o_ref[0] = x_ref[0] + m[2:3] * mix


def proj_out(x, mod, mod_row, o_f, o_b, u_gla, gla_norm, yf, yb, bonus, gate, ln_g, ln_b, w_out, gla_heads, rwkv_n,
             tm=256):
    b, t, d = x.shape
    gw = o_f.shape[-1]
    rw = yf.shape[-1]
    tm = min(tm, t)
    bmean = np.kron(np.eye(rw // rwkv_n, dtype=np.float32), np.full((rwkv_n, rwkv_n), 1.0 / rwkv_n, np.float32))
    tile = lambda wdt: pl.BlockSpec((1, tm, wdt), lambda i, j: (i, j, 0))
    row = lambda wdt: pl.BlockSpec((1, wdt), lambda i, j: (0, 0))
    g_col_block = (u_gla.shape[-1] - LANES - gw) // gw
    return pl.pallas_call(
        functools.partial(_proj_out_kernel, gla_heads=gla_heads),
        grid=(b, t // tm),
        in_specs=[tile(d),
                  pl.BlockSpec((1, 6, d), lambda i, j: (mod_row(i), 0, 0)),
                  tile(gw), tile(gw),
                  pl.BlockSpec((1, tm, gw), lambda i, j: (i, j, g_col_block)),
                  row(gw), tile(rw), tile(rw), tile(rw), tile(rw), row(rw), row(rw),
                  pl.BlockSpec((rw, rw), lambda i, j: (0, 0)),
                  pl.BlockSpec(w_out.shape, lambda i, j: (0, 0))],
        out_specs=tile(d),
        out_shape=jax.ShapeDtypeStruct((b, t, d), F32),
        compiler_params=_cparams(("parallel", "parallel")),
    )(x, mod, o_f, o_b, u_gla, gla_norm.reshape(1, gw), yf, yb, bonus, gate, ln_g.reshape(1, rw),
      ln_b.reshape(1, rw), jnp.asarray(bmean), w_out)


MOE_SUB = 128
MOE_CAP = 32
MOE_SUPER = 1024


def _route_kernel(x_ref, mod_ref, g_ref, rw_ref, rb_ref, tri_ref, h_ref, gates_ref, pos_ref, *, n_exp):
    m = mod_ref[0]
    h = _rms_mod(x_ref[...], g_ref[...], m[4:5], m[3:4])
    h_ref[...] = h.astype(BF16)
    tm = h.shape[0]
    gsz = n_exp // N_GROUPS
    scores = _sigmoid(_dot(rw_ref[...], h, NT, precision=HI))
    sel = scores + rb_ref[...]
    neg = jnp.float32(-jnp.inf)
    s3 = sel.reshape(N_GROUPS, gsz, tm)
    idx = lax.broadcasted_iota(jnp.int32, s3.shape, 1)
    m1 = jnp.max(s3, axis=1, keepdims=True)
    first = jnp.min(jnp.where(s3 == m1, idx, gsz), axis=1, keepdims=True)
    m2 = jnp.max(jnp.where(idx == first, neg, s3), axis=1, keepdims=True)
    gs = (m1 + m2).reshape(N_GROUPS, tm)
    gidx = lax.broadcasted_iota(jnp.int32, gs.shape, 0)
    cnt = jnp.zeros(gs.shape, jnp.int32)
    for g in range(N_GROUPS):
        other = gs[g:g + 1]
        cnt = cnt + ((other > gs) | ((other == gs) & (g < gidx))).astype(jnp.int32)
    gkeep = jnp.broadcast_to((cnt < TOPK_GROUPS)[:, None, :], s3.shape).reshape(n_exp, tm)
    selm = jnp.where(gkeep, sel, neg)
    eidx = lax.broadcasted_iota(jnp.int32, selm.shape, 0)
    rank = jnp.zeros(selm.shape, jnp.int32)
    for e in range(n_exp):
        other = selm[e:e + 1]
        rank = rank + ((other > selm) | ((other == selm) & (e < eidx))).astype(jnp.int32)
    chosen = rank < TOP_K
    wts = jnp.where(chosen, scores, 0.0)
    gates_ref[...] = wts / jnp.sum(wts, axis=0, keepdims=True) * ROUTED_SCALE
    pos_ref[...] = _dot(chosen.astype(BF16), tri_ref[...])


def route(x2, mod, mod_row, norm_g, router_w, router_b, tm):
    nt, d = x2.shape
    n_exp = router_w.shape[1]
    tri = np.triu(np.ones((tm, tm), np.float32), 1)
    return pl.pallas_call(
        functools.partial(_route_kernel, n_exp=n_exp),
        grid=(nt // tm,),
        in_specs=[pl.BlockSpec((tm, d), lambda i: (i, 0)),
                  pl.BlockSpec((1, 6, d), lambda i: (mod_row(i, tm), 0, 0)),
                  pl.BlockSpec((1, d), lambda i: (0, 0)),
                  pl.BlockSpec((n_exp, d), lambda i: (0, 0)),
                  pl.BlockSpec((n_exp, 1), lambda i: (0, 0)),
                  pl.BlockSpec((tm, tm), lambda i: (0, 0))],
        out_specs=[pl.BlockSpec((tm, d), lambda i: (i, 0)),
                   pl.BlockSpec((n_exp, tm), lambda i: (0, i)),
                   pl.BlockSpec((n_exp, tm), lambda i: (0, i))],
        out_shape=[jax.ShapeDtypeStruct((nt, d), BF16),
                   jax.ShapeDtypeStruct((n_exp, nt), F32),
                   jax.ShapeDtypeStruct((n_exp, nt), F32)],
        compiler_params=_cparams(("parallel",)),
    )(x2, mod, norm_g.reshape(1, d), router_w.T, router_b.reshape(n_exp, 1), jnp.asarray(tri, BF16))


def _swiglu(h, w1, w3, w2):
    return _bdot(_silu(_bdot(h, w1)) * _bdot(h, w3), w2)


def _moe_kernel(npass_ref, x_ref, mod_ref, h_ref, gates_ref, pos_ref, w1_ref, w3_ref, w2_ref, s1_ref, s3_ref, s2_ref,
                o_ref, xe_ref, ye_ref, *, sub, cap, group):
    i = pl.program_id(0)
    j = pl.program_id(1)
    n_grp = pl.num_programs(1)
    n_sub = h_ref.shape[0] // sub

    @pl.when(j == 0)
    def _():
        o_ref[...] = _swiglu(h_ref[...], s1_ref[...], s3_ref[...], s2_ref[...])

    gate = [gates_ref[pl.ds(j * group + x, 1), :] for x in range(group)]
    pos = [pos_ref[pl.ds(j * group + x, 1), :] for x in range(group)]
    rows = lax.broadcasted_iota(jnp.int32, (cap, sub), 0).astype(F32)

    def one_pass(it, carry):
        base = (it * cap).astype(F32)
        picks = []
        for x in range(group):
            picks.append([])
            for s in range(n_sub):
                sl = slice(s * sub, (s + 1) * sub)
                g = gate[x][:, sl]
                pick = jnp.where((pos[x][:, sl] - base == rows) & (g > 0.0), g, 0.0)
                picks[x].append(pick.astype(BF16))
                onehot = (pick > 0.0).astype(BF16)
                xe_ref[x, s * cap:(s + 1) * cap, :] = _dot(onehot, h_ref[sl, :]).astype(BF16)
            ye_ref[x] = _swiglu(xe_ref[x], w1_ref[x], w3_ref[x], w2_ref[x]).astype(BF16)
        for s in range(n_sub):
            sl = slice(s * sub, (s + 1) * sub)
            pick_all = jnp.concatenate([picks[x][s] for x in range(group)], axis=0)
            ye_all = jnp.concatenate([ye_ref[x, s * cap:(s + 1) * cap, :] for x in range(group)], axis=0)
            o_ref[sl, :] += _dot(pick_all, ye_all, TN)
        return carry

    lax.fori_loop(0, npass_ref[i * n_grp + j], one_pass, 0)

    @pl.when(j == n_grp - 1)
    def _():
        o_ref[...] = x_ref[...] + mod_ref[0][5:6] * o_ref[...]


def moe(x2, mod, mod_row, h, gates, pos, w1, w3, w2, s1, s3, s2, sub, tm):
    nt, d = x2.shape
    n_exp, _, ff = w1.shape
    cap = min(MOE_CAP, sub)
    group = LANES // cap
    n_sub = tm // sub
    cnt = (pos + (gates > 0.0))[:, sub - 1::sub].reshape(n_exp // group, group, nt // tm, n_sub).max(axis=(1, 3))
    npass = ((cnt.astype(jnp.int32) + cap - 1) // cap).T.reshape(-1)
    full = lambda a: pl.BlockSpec(a.shape, lambda i, j, n: (0,) * a.ndim)
    grid_spec = pltpu.PrefetchScalarGridSpec(
        num_scalar_prefetch=1,
        grid=(nt // tm, n_exp // group),
        in_specs=[pl.BlockSpec((tm, d), lambda i, j, n: (i, 0)),
                  pl.BlockSpec((1, 6, d), lambda i, j, n: (mod_row(i, tm), 0, 0)),
                  pl.BlockSpec((tm, d), lambda i, j, n: (i, 0)),
                  pl.BlockSpec((n_exp, tm), lambda i, j, n: (0, i)),
                  pl.BlockSpec((n_exp, tm), lambda i, j, n: (0, i)),
                  pl.BlockSpec((group, d, ff), lambda i, j, n: (j, 0, 0)),
                  pl.BlockSpec((group, d, ff), lambda i, j, n: (j, 0, 0)),
                  pl.BlockSpec((group, ff, d), lambda i, j, n: (j, 0, 0)),
                  full(s1), full(s3), full(s2)],
        out_specs=pl.BlockSpec((tm, d), lambda i, j, n: (i, 0)),
        scratch_shapes=[pltpu.VMEM((group, n_sub * cap, d), BF16), pltpu.VMEM((group, n_sub * cap, d), BF16)])
    return pl.pallas_call(
        functools.partial(_moe_kernel, sub=sub, cap=cap, group=group),
        grid_spec=grid_spec,
        out_shape=jax.ShapeDtypeStruct((nt, d), F32),
        compiler_params=_cparams(("parallel", "arbitrary")),
    )(npass, x2, mod, h, gates, pos, w1, w3, w2, s1, s3, s2)


def _final_kernel(x_ref, g_ref, o_ref):
    x = x_ref[...]
    o_ref[...] = x * lax.rsqrt(jnp.mean(x * x, axis=-1, keepdims=True) + EPS) * g_ref[...]


def final_norm(x2, g, tm=512):
    nt, d = x2.shape
    tm = min(tm, nt)
    return pl.pallas_call(
        _final_kernel,
        grid=(nt // tm,),
        in_specs=[pl.BlockSpec((tm, d), lambda i: (i, 0)), pl.BlockSpec((1, d), lambda i: (0, 0))],
        out_specs=pl.BlockSpec((tm, d), lambda i: (i, 0)),
        out_shape=jax.ShapeDtypeStruct((nt, d), F32),
        compiler_params=_cparams(("parallel",)),
    )(x2, g.reshape(1, d))


def _pad_cols(a, n):
    return jnp.pad(a, [(0, 0)] * (a.ndim - 1) + [(0, n - a.shape[-1])])


def _round_up(n, m):
    return -(-n // m) * m


def _layer(x, mod, latent, s_gla_c, s_rwkv, p, dims, grid_w):
    b, t, d = x.shape
    ctx_row = mod.shape[0] - 1
    brow = (lambda i: i) if latent else (lambda i: ctx_row)
    trow = (lambda i, tm: (i * tm) // t) if latent else (lambda i, tm: ctx_row)
    u_gla, u_rwkv = proj_in(x, mod, brow, p['norm1'], p['w_in'], dims['gla_cols'])
    hh, ww = (t // grid_w, grid_w) if latent else (b, t)
    u_rwkv = conv(u_rwkv, p['rwkv_conv'], hh, ww, vertical=latent)
    o_f, o_b, sg = gla(u_gla, s_gla_c, p['gla_w_dec'], p['gla_b_dec'], dims['gla_heads'], dims['gla_dk'], dims['gla_dv'])
    yf, yb, bonus, gate, sr = rwkv(u_rwkv, s_rwkv, p['rwkv_w2'], p['rwkv_w0'], p['rwkv_a2'], p['rwkv_a0'],
                                   p['rwkv_g2'], p['rwkv_k_k'], p['rwkv_k_a'], p['rwkv_r_k'],
                                   dims['rwkv_heads'], dims['rwkv_n'])
    x = proj_out(x, mod, brow, o_f, o_b, u_gla, p['gla_norm'], yf, yb, bonus, gate, p['rwkv_ln_g'], p['rwkv_ln_b'],
                 p['w_out'], dims['gla_heads'], dims['rwkv_n'])
    x2 = x.reshape(b * t, d)
    sub = min(MOE_SUB, t if latent else b * t)
    tm = min(MOE_SUPER, t if latent else b * t)
    h, gates, pos = route(x2, mod, trow, p['norm2'], p['router_w'], p['router_b'], sub)
    x2 = moe(x2, mod, trow, h, gates, pos, p['exp_w1'], p['exp_w3'], p['exp_w2'], p['sh_w1'], p['sh_w3'], p['sh_w2'],
             sub, tm)
    return x2.reshape(b, t, d), sg, sr


def kernel(x_prompt, x_sample, c, state_gla, state_rwkv, c_ctx, ada_w, ada_b, norm1, norm2, norm_f, w_in, w_out,
           gla_w_dec, gla_b_dec, gla_norm, rwkv_conv, rwkv_w2, rwkv_w0, rwkv_a2, rwkv_a0, rwkv_g2, rwkv_k_k, rwkv_k_a,
           rwkv_r_k, rwkv_ln_g, rwkv_ln_b, router_w, router_b, exp_w1, exp_w3, exp_w2, sh_w1, sh_w3, sh_w2):
    depth, d, _ = ada_w.shape
    bp = x_prompt.shape[0]
    bs = x_sample.shape[0]
    grid_w = 64
    gla_heads, gla_dk, gla_dv = state_gla.shape[3:]
    rwkv_heads, rwkv_n = state_rwkv.shape[3:5]
    gla_qk, gla_w, rwkv_w = gla_heads * gla_dk, gla_heads * gla_dv, rwkv_heads * rwkv_n
    gla_cols_raw = 2 * gla_qk + 2 * gla_w + 2 * gla_w_dec.shape[2]
    rwkv_cols_raw = w_in.shape[2] - gla_cols_raw
    gla_cols = 2 * gla_qk + 2 * gla_w + LANES
    rwkv_cols = _round_up(rwkv_cols_raw, LANES)
    dims = dict(gla_cols=gla_cols, gla_heads=gla_heads, gla_dk=gla_dk, gla_dv=gla_dv,
                rwkv_heads=rwkv_heads, rwkv_n=rwkv_n)

    w_in_p = jnp.concatenate([_pad_cols(w_in[..., :gla_cols_raw], gla_cols),
                              _pad_cols(w_in[..., gla_cols_raw:], rwkv_cols)], axis=-1).astype(BF16)
    conv_p = _pad_cols(rwkv_conv, rwkv_cols)
    w_out_b = w_out.astype(BF16)
    e1, e3, e2 = exp_w1.astype(BF16), exp_w3.astype(BF16), exp_w2.astype(BF16)
    s1, s3, s2 = sh_w1.astype(BF16), sh_w3.astype(BF16), sh_w2.astype(BF16)

    rows = _round_up(bs + 1, 8)
    cvec = jnp.concatenate([c, c_ctx[None], jnp.zeros((rows - bs - 1, d), F32)], axis=0)
    mod_all = adaln(cvec, ada_w, ada_b).reshape(depth, rows, 6, d)[:, :bs + 1]

    zero_gla = jnp.zeros((bp, 2, gla_dv, gla_qk), F32)
    zero_rwkv = jnp.zeros((bp, 2, rwkv_heads, rwkv_n, rwkv_n), F32)
    state_gla_c = _gla_state_in(state_gla)

    xp, xs = x_prompt, x_sample
    new_gla, new_rwkv = [], []
    for l in range(depth):
        p = dict(norm1=norm1[l], norm2=norm2[l], w_in=w_in_p[l], w_out=w_out_b[l], gla_w_dec=gla_w_dec[l],
                 gla_b_dec=gla_b_dec[l], gla_norm=gla_norm[l], rwkv_conv=conv_p[l], rwkv_w2=rwkv_w2[l],
                 rwkv_w0=rwkv_w0[l], rwkv_a2=rwkv_a2[l], rwkv_a0=rwkv_a0[l], rwkv_g2=rwkv_g2[l],
                 rwkv_k_k=rwkv_k_k[l], rwkv_k_a=rwkv_k_a[l], rwkv_r_k=rwkv_r_k[l], rwkv_ln_g=rwkv_ln_g[l],
                 rwkv_ln_b=rwkv_ln_b[l], router_w=router_w[l], router_b=router_b[l], exp_w1=e1[l], exp_w3=e3[l],
                 exp_w2=e2[l], sh_w1=s1[l], sh_w3=s3[l], sh_w2=s2[l])
        mod = mod_all[l]
        xp, sg, sr = _layer(xp, mod, False, zero_gla, zero_rwkv, p, dims, grid_w)
        new_gla.append(_gla_state_out(sg, gla_heads))
        new_rwkv.append(sr)
        xs, _, _ = _layer(xs, mod, True, state_gla_c[:, l], state_rwkv[:, l], p, dims, grid_w)
    y_prompt = final_norm(xp.reshape(-1, d), norm_f).reshape(xp.shape)
    y_sample = final_norm(xs.reshape(-1, d), norm_f).reshape(xs.shape)
    return (y_prompt, y_sample, jnp.stack(new_gla, axis=1), jnp.stack(new_rwkv, axis=1))
```

```python
import functools

import numpy as np
import jax
import jax.numpy as jnp
from jax import lax
from jax.experimental import pallas as pl
from jax.experimental.pallas import tpu as pltpu

F32 = jnp.float32
BF16 = jnp.bfloat16
HI = lax.Precision.HIGHEST

LANES = 128
EPS = 1e-6
RWKV_LN_EPS = 64e-5
GLA_NORMALIZER = 16.0
ROUTED_SCALE = 2.5
N_GROUPS = 8
TOPK_GROUPS = 4
TOP_K = 8
CHUNK = 64
VMEM_LIMIT = 56 * 1024 * 1024


def _cparams(sem):
    return pltpu.CompilerParams(dimension_semantics=sem, vmem_limit_bytes=VMEM_LIMIT)


def _dot(a, b, dims=(((1,), (0,)), ((), ())), precision=None):
    return lax.dot_general(a, b, dims, precision=precision, preferred_element_type=F32)


def _bdot(a, b, dims=(((1,), (0,)), ((), ()))):
    return lax.dot_general(a.astype(BF16), b.astype(BF16), dims, preferred_element_type=F32)


NT = (((1,), (1,)), ((), ()))
TN = (((0,), (0,)), ((), ()))
BNN = (((2,), (1,)), ((0,), (0,)))
BNT = (((2,), (2,)), ((0,), (0,)))


def _bmm(a, b, dims=BNN):
    return lax.dot_general(a.astype(BF16), b.astype(BF16), dims, preferred_element_type=F32)


def _two_terms(x):
    hi = x.astype(BF16)
    return hi, (x - hi.astype(F32)).astype(BF16)


def _split_dot(x, m):
    hi, lo = _two_terms(x)
    mb = m.astype(BF16)
    return _dot(hi, mb) + _dot(lo, mb)


def _split_dot_l(m, x):
    hi, lo = _two_terms(x)
    mb = m.astype(BF16)
    return _dot(mb, hi) + _dot(mb, lo)


def _sigmoid(x):
    return 1.0 / (1.0 + jnp.exp(-x))


def _silu(x):
    return x * _sigmoid(x)


def _softplus(x):
    return jnp.maximum(x, 0.0) + jnp.log(1.0 + jnp.exp(-jnp.abs(x)))


def _log_sigmoid(x):
    return -_softplus(-x)


def _rms_mod(x, g, scale, shift):
    y = x * lax.rsqrt(jnp.mean(x * x, axis=-1, keepdims=True) + EPS)
    return (y * g) * (1.0 + scale) + shift


def _adaln_kernel(c_ref, w_ref, b_ref, o_ref):
    c = c_ref[...]
    o_ref[0] = _dot(_silu(c), w_ref[0], precision=HI) + b_ref[0]


def adaln(cvec, ada_w, ada_b, tn=512):
    depth, d, n = ada_w.shape
    r = cvec.shape[0]
    return pl.pallas_call(
        _adaln_kernel,
        grid=(depth, n // tn),
        in_specs=[pl.BlockSpec((r, d), lambda l, j: (0, 0)),
                  pl.BlockSpec((1, d, tn), lambda l, j: (l, 0, j)),
                  pl.BlockSpec((1, 1, tn), lambda l, j: (l, 0, j))],
        out_specs=pl.BlockSpec((1, r, tn), lambda l, j: (l, 0, j)),
        out_shape=jax.ShapeDtypeStruct((depth, r, n), F32),
        compiler_params=_cparams(("parallel", "parallel")),
    )(cvec, ada_w, ada_b.reshape(depth, 1, n))


def _proj_in_kernel(x_ref, mod_ref, g_ref, w_ref, og_ref, or_ref, *, n_gla):
    m = mod_ref[0]
    h = _rms_mod(x_ref[0], g_ref[...], m[1:2], m[0:1])
    u = _bdot(h, w_ref[...])
    og_ref[0] = u[:, :n_gla]
    or_ref[0] = u[:, n_gla:]


def proj_in(x, mod, mod_row, norm_g, w, n_gla, tm=256):
    b, t, d = x.shape
    n = w.shape[1]
    tm = min(tm, t)
    return pl.pallas_call(
        functools.partial(_proj_in_kernel, n_gla=n_gla),
        grid=(b, t // tm),
        in_specs=[pl.BlockSpec((1, tm, d), lambda i, j: (i, j, 0)),
                  pl.BlockSpec((1, 6, d), lambda i, j: (mod_row(i), 0, 0)),
                  pl.BlockSpec((1, d), lambda i, j: (0, 0)),
                  pl.BlockSpec((d, n), lambda i, j: (0, 0))],
        out_specs=[pl.BlockSpec((1, tm, n_gla), lambda i, j: (i, j, 0)),
                   pl.BlockSpec((1, tm, n - n_gla), lambda i, j: (i, j, 0))],
        out_shape=[jax.ShapeDtypeStruct((b, t, n_gla), F32),
                   jax.ShapeDtypeStruct((b, t, n - n_gla), F32)],
        compiler_params=_cparams(("parallel", "parallel")),
    )(x, mod, norm_g.reshape(1, d), w)


CONV_PAD = 8


def _conv_kernel(u_ref, w_ref, o_ref, pad_ref, *, hh, ww, vertical):
    pad_ref[...] = jnp.zeros(pad_ref.shape, F32)
    pad_ref[1:hh + 1, CONV_PAD:CONV_PAD + ww, :] = u_ref[0]
    w = w_ref[...]
    acc = jnp.zeros((hh, ww, u_ref.shape[-1]), F32)
    for dy in range(3):
        if not vertical and dy != 1:
            continue
        for dx in range(3):
            acc = acc + pad_ref[dy:dy + hh, CONV_PAD - 1 + dx:CONV_PAD - 1 + dx + ww, :] * w[dy, dx]
    o_ref[0] = acc


def conv(u, wconv, hh, ww, vertical):
    c = u.shape[-1]
    b = u.size // (hh * ww * c)
    out = pl.pallas_call(
        functools.partial(_conv_kernel, hh=hh, ww=ww, vertical=vertical),
        grid=(b, c // LANES),
        in_specs=[pl.BlockSpec((1, hh, ww, LANES), lambda i, j: (i, 0, 0, j)),
                  pl.BlockSpec((3, 3, LANES), lambda i, j: (0, 0, j))],
        out_specs=pl.BlockSpec((1, hh, ww, LANES), lambda i, j: (i, 0, 0, j)),
        out_shape=jax.ShapeDtypeStruct((b, hh, ww, c), F32),
        scratch_shapes=[pltpu.VMEM((hh + 2, ww + 2 * CONV_PAD, LANES), F32)],
        compiler_params=_cparams(("parallel", "parallel")),
    )(u.reshape(b, hh, ww, c), wconv)
    return out.reshape(u.shape)


GLA_LEVELS = (32, 16, 8, 4, 2)


def _order_consts(rev):
    c = CHUNK
    idx = np.arange(c)
    pos = (c - 1 - idx) if rev else idx
    incl = (pos[None, :] <= pos[:, None])
    strict = (pos[None, :] < pos[:, None])
    return pos, incl, strict


def _gla_consts():
    c = CHUNK
    mats, masks = [], []
    for rev in (False, True):
        pos, incl, strict = _order_consts(rev)
        later = (pos[None, :] > pos[:, None])
        rows = [incl, later]
        pins, sxs, lm = [], [], [np.eye(c, dtype=bool)]
        for s in (1,) + GLA_LEVELS:
            blk = pos // s
            same = blk[None, :] == blk[:, None]
            if s > 1:
                pins.append(same & incl)
                sxs.append(same & later)
            lm.append(((blk[:, None] % 2 == 1) & (blk[None, :] == blk[:, None] - 1)).T)
        mats.append(np.concatenate(rows + pins + sxs, axis=0).astype(np.float32))
        masks.append(np.stack(lm).astype(np.float32))
    return np.stack(mats), np.stack(masks)


def _rwkv_consts():
    c = CHUNK
    out = []
    for rev in (False, True):
        pos, incl, strict = _order_consts(rev)
        b16 = pos // 16
        b32 = pos // 32
        same16 = b16[None, :] == b16[:, None]
        same32 = b32[None, :] == b32[:, None]
        out.append(np.stack([incl, strict, strict & same16, strict & same32 & ~same16,
                             strict & ~same32, np.eye(c, dtype=bool)]).astype(np.float32))
    return np.stack(out)


def _gla_kernel(uf_ref, ub_ref, s0_ref, cst_ref, msk_ref, hm_ref, wd_ref, bd_ref, of_ref, ob_ref, sf_ref, s_scr,
                *, heads, dk, dv):
    ci = pl.program_id(1)

    @pl.when(ci == 0)
    def _():
        s_scr[...] = s0_ref[0]

    c = CHUNK
    qk = heads * dk
    nl = len(GLA_LEVELS)
    hm = hm_ref[...]

    def rows_per_head(x):
        return jnp.concatenate([x] * heads, axis=0)

    dirs = (0, 1)
    us = [uf_ref[0], ub_ref[0]]
    q = [u[:, 0:qk] * (dk ** -0.5) for u in us]
    k = [u[:, qk:2 * qk] for u in us]
    vb = [u[:, 2 * qk:2 * qk + heads * dv].astype(BF16) for u in us]
    gk = [_log_sigmoid(_dot(us[d][:, 2 * qk + 2 * heads * dv:], wd_ref[d], precision=HI) + bd_ref[d])
          * (1.0 / GLA_NORMALIZER) for d in dirs]
    e = [jnp.exp(_split_dot_l(cst_ref[d], gk[d])) for d in dirs]
    qm = [rows_per_head(q[d]) * hm for d in dirs]
    kb = [k[d].astype(BF16) for d in dirs]
    attn = [_dot(kb[d], qm[d].astype(BF16), NT) * msk_ref[d, 0] for d in dirs]
    attn = [attn[d] + _dot(kb[d], (qm[d] * rows_per_head(jnp.exp(gk[d]))).astype(BF16), NT) * msk_ref[d, 1]
            for d in dirs]
    for li in range(nl):
        for d in dirs:
            qs = qm[d] * rows_per_head(e[d][(2 + li) * c:(3 + li) * c])
            ks = k[d] * e[d][(2 + nl + li) * c:(3 + nl + li) * c]
            attn[d] = attn[d] + _dot(ks.astype(BF16), qs.astype(BF16), NT) * msk_ref[d, 2 + li]
    st = [s_scr[d] for d in dirs]
    o_inter = [_dot((qm[d] * rows_per_head(e[d][0:c])).astype(BF16), st[d].astype(BF16), NT) for d in dirs]
    o_intra = [_dot(attn[d].astype(BF16), vb[d], TN) for d in dirs]
    for d, o_ref in ((0, of_ref), (1, ob_ref)):
        o_ref[0] = jnp.concatenate(
            [o_inter[d][h * c:(h + 1) * c] + o_intra[d][h * c:(h + 1) * c, h * dv:(h + 1) * dv]
             for h in range(heads)], axis=-1)
    for d in dirs:
        etot = jnp.exp(jnp.sum(gk[d], axis=0, keepdims=True))
        vstack = jnp.concatenate([vb[d][:, h * dv:(h + 1) * dv] for h in range(heads)], axis=0)
        k_rest = (rows_per_head(k[d] * e[d][c:2 * c]) * hm).astype(BF16)
        s_scr[d] = st[d] * etot + _dot(vstack, k_rest, TN)

    @pl.when(ci == pl.num_programs(1) - 1)
    def _():
        sf_ref[0] = s_scr[...]


def gla(u_gla, s0c, w_dec, b_dec, heads, dk, dv):
    b, t, cols = u_gla.shape
    nc = t // CHUNK
    qk = heads * dk
    rank = w_dec.shape[1]
    cst, msk = _gla_consts()
    msk = np.tile(msk, (1, 1, 1, heads))
    hm = np.kron(np.eye(heads, dtype=np.float32), np.ones((CHUNK, dk), np.float32))
    wd = jnp.zeros((2, LANES, qk), F32)
    wd = wd.at[0, 0:rank].set(w_dec[0]).at[1, rank:2 * rank].set(w_dec[1])
    full = lambda a: pl.BlockSpec(a.shape, lambda i, j: (0,) * a.ndim)
    consts = [jnp.asarray(cst), jnp.asarray(msk), jnp.asarray(hm), wd, b_dec.reshape(2, 1, qk)]
    return pl.pallas_call(
        functools.partial(_gla_kernel, heads=heads, dk=dk, dv=dv),
        grid=(b, nc),
        in_specs=[pl.BlockSpec((1, CHUNK, cols), lambda i, j: (i, j, 0)),
                  pl.BlockSpec((1, CHUNK, cols), lambda i, j: (i, nc - 1 - j, 0)),
                  pl.BlockSpec((1, 2, dv, qk), lambda i, j: (i, 0, 0, 0))] + [full(a) for a in consts],
        out_specs=[pl.BlockSpec((1, CHUNK, heads * dv), lambda i, j: (i, j, 0)),
                   pl.BlockSpec((1, CHUNK, heads * dv), lambda i, j: (i, nc - 1 - j, 0)),
                   pl.BlockSpec((1, 2, dv, qk), lambda i, j: (i, 0, 0, 0))],
        out_shape=[jax.ShapeDtypeStruct((b, t, heads * dv), F32),
                   jax.ShapeDtypeStruct((b, t, heads * dv), F32),
                   jax.ShapeDtypeStruct((b, 2, dv, qk), F32)],
        scratch_shapes=[pltpu.VMEM((2, dv, qk), F32)],
        compiler_params=_cparams(("parallel", "arbitrary")),
    )(u_gla, u_gla, s0c, *consts)


def _gla_state_in(s):
    h, dk, dv = s.shape[-3:]
    return jnp.moveaxis(s, -1, -3).reshape(s.shape[:-3] + (dv, h * dk))


def _gla_state_out(s, heads):
    dv, qk = s.shape[-2:]
    return jnp.moveaxis(s.reshape(s.shape[:-2] + (dv, heads, qk // heads)), -3, -1)


def _dot3(x, w_ref):
    hi, lo = _two_terms(x)
    return _dot(hi, w_ref[0]) + _dot(lo, w_ref[0]) + _dot(hi, w_ref[1])


def _rwkv_kernel(uf_ref, ub_ref, s0_ref, msk_ref, w2_ref, w0_ref, a2_ref, a0_ref, g2_ref, kk_ref, ka_ref, rk_ref,
                 bones_ref, yf_ref, yb_ref, bonus_ref, gate_ref, sf_ref, s_scr, *, heads, n):
    ci = pl.program_id(1)
    g2h = 2 * heads

    @pl.when(ci == 0)
    def _():
        s_scr[...] = s0_ref[0].reshape(g2h, n, n)

    c = CHUNK
    wd = heads * n
    bones = bones_ref[...]

    def split_heads(x):
        return jnp.stack([x[:, h * n:(h + 1) * n] for h in range(heads)], axis=0)

    def masked(x, mi):
        return jnp.concatenate([jnp.where(msk_ref[d, mi] > 0, x[d * heads:(d + 1) * heads], 0.0) for d in (0, 1)],
                               axis=0)

    ars, kbs, vvs, kbhs, e_tots = [], [], [], [], []
    for d, u_ref in ((0, uf_ref), (1, ub_ref)):
        u = u_ref[0]
        r, k, v = u[:, 0:wd], u[:, wd:2 * wd], u[:, 2 * wd:3 * wd]
        z0 = 3 * wd
        zw = u[:, z0 + d * n:z0 + (d + 1) * n]
        za = u[:, z0 + 2 * n:z0 + 3 * n]
        wlog = -_softplus(-(w0_ref[d] + _dot3(jnp.tanh(zw), w2_ref.at[d]))) - 0.5
        logw = -jnp.exp(wlog)
        a = _sigmoid(a0_ref[...] + _dot3(za, a2_ref))
        kk = k * kk_ref[...]
        kk = kk * lax.rsqrt(jnp.maximum(_split_dot(kk * kk, bones), 1e-24))
        km = k * (1.0 + (a - 1.0) * ka_ref[...])
        bb = kk * a
        cum = _split_dot_l(msk_ref[d, 0], logw)
        tot = jnp.sum(logw, axis=0, keepdims=True)
        e_neg = jnp.exp(-cum)
        e_rest = jnp.exp(tot - cum)
        e_tots.append(jnp.exp(tot))
        ars.append(split_heads(jnp.concatenate([kk * jnp.exp(cum - logw), r * jnp.exp(cum)], axis=0)))
        kbs.append(split_heads(jnp.concatenate([km * e_neg, bb * e_neg], axis=0)))
        vvs.append(split_heads(v))
        kbhs.append(jnp.concatenate([km * e_rest, bb * e_rest], axis=0))
        if d == 0:
            bonus_ref[0] = _split_dot(r * km * rk_ref[...], bones) * v
            zg = u[:, z0 + 3 * n:z0 + 3 * n + g2_ref.shape[1]]
            gate_ref[0] = _dot3(_sigmoid(zg), g2_ref)

    ar = jnp.concatenate(ars, axis=0)
    kb = jnp.concatenate(kbs, axis=0)
    vv = jnp.concatenate(vvs, axis=0)
    st = s_scr[...]
    sc = _bmm(ar, kb, BNT)
    lak = masked(sc[:, :c, :c], 1)
    lab = masked(sc[:, :c, c:], 1)
    mrk = masked(sc[:, c:, :c], 0)
    mrb = masked(sc[:, c:, c:], 0)
    through = _bmm(ar, st, BNT)
    rhs = through[:, :c] + _bmm(lak, vv)
    ld = masked(lab, 2)
    x = msk_ref[0, 5] - ld
    p = _bmm(ld, ld)
    x = x + _bmm(x, p)
    p = _bmm(p, p)
    x = x + _bmm(x, p)
    p = _bmm(p, p)
    x = x + _bmm(x, p)
    for mi in (3, 4):
        x = x - _bmm(x, _bmm(masked(lab, mi), x))
    uu = _bmm(x, rhs)
    vu = jnp.concatenate([vv, -uu], axis=1)
    y = through[:, c:] + _bmm(jnp.concatenate([mrk, mrb], axis=2), vu)
    yf_ref[0] = jnp.concatenate([y[h] for h in range(heads)], axis=-1)
    yb_ref[0] = jnp.concatenate([y[heads + h] for h in range(heads)], axis=-1)
    for d in (0, 1):
        for h in range(heads):
            sl = slice(h * n, (h + 1) * n)
            g = d * heads + h
            s_scr[g] = st[g] * e_tots[d][:, sl] + _bdot(vu[g], kbhs[d][:, sl], TN)

    @pl.when(ci == pl.num_programs(1) - 1)
    def _():
        sf_ref[0] = s_scr[...].reshape(2, heads, n, n)


def rwkv(u, s0, w2, w0, a2, a0, g2, k_k, k_a, r_k, heads, n):
    b, t, cols = u.shape
    nc = t // CHUNK
    wd = heads * n
    msk = _rwkv_consts()
    bones = np.kron(np.eye(heads, dtype=np.float32), np.ones((n, n), np.float32))
    row = lambda x: x.reshape(1, wd)
    full = lambda a: pl.BlockSpec(a.shape, lambda i, j: (0,) * a.ndim)
    terms = lambda w: jnp.stack(_two_terms(w), axis=-3)
    args = [jnp.asarray(msk), terms(w2), w0.reshape(2, 1, wd), terms(a2), row(a0), terms(g2), row(k_k), row(k_a),
            row(r_k), jnp.asarray(bones)]
    return pl.pallas_call(
        functools.partial(_rwkv_kernel, heads=heads, n=n),
        grid=(b, nc),
        in_specs=[pl.BlockSpec((1, CHUNK, cols), lambda i, j: (i, j, 0)),
                  pl.BlockSpec((1, CHUNK, cols), lambda i, j: (i, nc - 1 - j, 0)),
                  pl.BlockSpec((1, 2, heads, n, n), lambda i, j: (i, 0, 0, 0, 0))] + [full(a) for a in args],
        out_specs=[pl.BlockSpec((1, CHUNK, wd), lambda i, j: (i, j, 0)),
                   pl.BlockSpec((1, CHUNK, wd), lambda i, j: (i, nc - 1 - j, 0)),
                   pl.BlockSpec((1, CHUNK, wd), lambda i, j: (i, j, 0)),
                   pl.BlockSpec((1, CHUNK, wd), lambda i, j: (i, j, 0)),
                   pl.BlockSpec((1, 2, heads, n, n), lambda i, j: (i, 0, 0, 0, 0))],
        out_shape=[jax.ShapeDtypeStruct((b, t, wd), F32)] * 4
        + [jax.ShapeDtypeStruct((b, 2, heads, n, n), F32)],
        scratch_shapes=[pltpu.VMEM((2 * heads, n, n), F32)],
        compiler_params=_cparams(("parallel", "arbitrary")),
    )(u, u, s0, *args)


def _proj_out_kernel(x_ref, mod_ref, of_ref, ob_ref, gg_ref, gn_ref, yf_ref, yb_ref, bonus_ref, gate_ref,
                     lng_ref, lnb_ref, bmean_ref, w_ref, o_ref, *, gla_heads):
    m = mod_ref[0]
    o = of_ref[0] + ob_ref[0]
    gw = o.shape[-1]
    dv = gw // gla_heads
    parts = []
    for h in range(gla_heads):
        oh = o[:, h * dv:(h + 1) * dv]
        parts.append(oh * lax.rsqrt(jnp.mean(oh * oh, axis=-1, keepdims=True) + EPS))
    o_gla = jnp.concatenate(parts, axis=-1) * gn_ref[...] * _silu(gg_ref[0])
    y = yf_ref[0] + yb_ref[0]
    bmean = bmean_ref[...]
    dy = y - _split_dot(y, bmean)
    var = _split_dot(dy * dy, bmean)
    yn = dy * lax.rsqrt(var + RWKV_LN_EPS) * lng_ref[...] + lnb_ref[...]
    o_rwkv = (yn + bonus_ref[0]) * gate_ref[0]
    w = w_ref[...]
    mix = _bdot(o_gla, w[:gw]) + _bdot(o_rwkv, w[gw:])
    o_ref[0] = x_ref[0] + m[2:3] * mix


def proj_out(x, mod, mod_row, o_f, o_b, u_gla, gla_norm, yf, yb, bonus, gate, ln_g, ln_b, w_out, gla_heads, rwkv_n,
             tm=256):
    b, t, d = x.shape
    gw = o_f.shape[-1]
    rw = yf.shape[-1]
    tm = min(tm, t)
    bmean = np.kron(np.eye(rw // rwkv_n, dtype=np.float32), np.full((rwkv_n, rwkv_n), 1.0 / rwkv_n, np.float32))
    tile = lambda wdt: pl.BlockSpec((1, tm, wdt), lambda i, j: (i, j, 0))
    row = lambda wdt: pl.BlockSpec((1, wdt), lambda i, j: (0, 0))
    g_col_block = (u_gla.shape[-1] - LANES - gw) // gw
    return pl.pallas_call(
        functools.partial(_proj_out_kernel, gla_heads=gla_heads),
        grid=(b, t // tm),
        in_specs=[tile(d),
                  pl.BlockSpec((1, 6, d), lambda i, j: (mod_row(i), 0, 0)),
                  tile(gw), tile(gw),
                  pl.BlockSpec((1, tm, gw), lambda i, j: (i, j, g_col_block)),
                  row(gw), tile(rw), tile(rw), tile(rw), tile(rw), row(rw), row(rw),
                  pl.BlockSpec((rw, rw), lambda i, j: (0, 0)),
                  pl.BlockSpec(w_out.shape, lambda i, j: (0, 0))],
        out_specs=tile(d),
        out_shape=jax.ShapeDtypeStruct((b, t, d), F32),
        compiler_params=_cparams(("parallel", "parallel")),
    )(x, mod, o_f, o_b, u_gla, gla_norm.reshape(1, gw), yf, yb, bonus, gate, ln_g.reshape(1, rw),
      ln_b.reshape(1, rw), jnp.asarray(bmean), w_out)


MOE_SUB = 128
MOE_CAP = 32
MOE_SUPER = 1024


def _route_kernel(x_ref, mod_ref, g_ref, rw_ref, rb_ref, tri_ref, h_ref, gates_ref, pos_ref, *, n_exp):
    m = mod_ref[0]
    h = _rms_mod(x_ref[...], g_ref[...], m[4:5], m[3:4])
    h_ref[...] = h.astype(BF16)
    tm = h.shape[0]
    gsz = n_exp // N_GROUPS
    scores = _sigmoid(_dot(rw_ref[...], h, NT, precision=HI))
    sel = scores + rb_ref[...]
    neg = jnp.float32(-jnp.inf)
    s3 = sel.reshape(N_GROUPS, gsz, tm)
    idx = lax.broadcasted_iota(jnp.int32, s3.shape, 1)
    m1 = jnp.max(s3, axis=1, keepdims=True)
    first = jnp.min(jnp.where(s3 == m1, idx, gsz), axis=1, keepdims=True)
    m2 = jnp.max(jnp.where(idx == first, neg, s3), axis=1, keepdims=True)
    gs = (m1 + m2).reshape(N_GROUPS, tm)
    gidx = lax.broadcasted_iota(jnp.int32, gs.shape, 0)
    cnt = jnp.zeros(gs.shape, jnp.int32)
    for g in range(N_GROUPS):
        other = gs[g:g + 1]
        cnt = cnt + ((other > gs) | ((other == gs) & (g < gidx))).astype(jnp.int32)
    gkeep = jnp.broadcast_to((cnt < TOPK_GROUPS)[:, None, :], s3.shape).reshape(n_exp, tm)
    selm = jnp.where(gkeep, sel, neg)
    eidx = lax.broadcasted_iota(jnp.int32, selm.shape, 0)
    rank = jnp.zeros(selm.shape, jnp.int32)
    for e in range(n_exp):
        other = selm[e:e + 1]
        rank = rank + ((other > selm) | ((other == selm) & (e < eidx))).astype(jnp.int32)
    chosen = rank < TOP_K
    wts = jnp.where(chosen, scores, 0.0)
    gates_ref[...] = wts / jnp.sum(wts, axis=0, keepdims=True) * ROUTED_SCALE
    pos_ref[...] = _dot(chosen.astype(BF16), tri_ref[...])


def route(x2, mod, mod_row, norm_g, router_w, router_b, tm):
    nt, d = x2.shape
    n_exp = router_w.shape[1]
    tri = np.triu(np.ones((tm, tm), np.float32), 1)
    return pl.pallas_call(
        functools.partial(_route_kernel, n_exp=n_exp),
        grid=(nt // tm,),
        in_specs=[pl.BlockSpec((tm, d), lambda i: (i, 0)),
                  pl.BlockSpec((1, 6, d), lambda i: (mod_row(i, tm), 0, 0)),
                  pl.BlockSpec((1, d), lambda i: (0, 0)),
                  pl.BlockSpec((n_exp, d), lambda i: (0, 0)),
                  pl.BlockSpec((n_exp, 1), lambda i: (0, 0)),
                  pl.BlockSpec((tm, tm), lambda i: (0, 0))],
        out_specs=[pl.BlockSpec((tm, d), lambda i: (i, 0)),
                   pl.BlockSpec((n_exp, tm), lambda i: (0, i)),
                   pl.BlockSpec((n_exp, tm), lambda i: (0, i))],
        out_shape=[jax.ShapeDtypeStruct((nt, d), BF16),
                   jax.ShapeDtypeStruct((n_exp, nt), F32),
                   jax.ShapeDtypeStruct((n_exp, nt), F32)],
        compiler_params=_cparams(("parallel",)),
    )(x2, mod, norm_g.reshape(1, d), router_w.T, router_b.reshape(n_exp, 1), jnp.asarray(tri, BF16))


def _swiglu(h, w1, w3, w2):
    return _bdot(_silu(_bdot(h, w1)) * _bdot(h, w3), w2)


def _moe_kernel(npass_ref, x_ref, mod_ref, h_ref, gates_ref, pos_ref, w1_ref, w3_ref, w2_ref, s1_ref, s3_ref, s2_ref,
                o_ref, xe_ref, ye_ref, *, sub, cap, group):
    i = pl.program_id(0)
    j = pl.program_id(1)
    n_grp = pl.num_programs(1)
    n_sub = h_ref.shape[0] // sub

    @pl.when(j == 0)
    def _():
        o_ref[...] = _swiglu(h_ref[...], s1_ref[...], s3_ref[...], s2_ref[...])

    first = (i * n_grp + j) * group
    gate = [gates_ref[pl.ds(j * group + x, 1), :] for x in range(group)]
    pos = [pos_ref[pl.ds(j * group + x, 1), :] for x in range(group)]
    need = [npass_ref[first + x] for x in range(group)]
    rows = lax.broadcasted_iota(jnp.int32, (cap, sub), 0).astype(F32)

    def one_pass(it, skip_done):
        base = it * float(cap) if isinstance(it, int) else (it * cap).astype(F32)
        picks = []
        for x in range(group):
            picks.append([])
            for s in range(n_sub):
                sl = slice(s * sub, (s + 1) * sub)
                g = gate[x][:, sl]
                pick = jnp.where((pos[x][:, sl] - base == rows) & (g > 0.0), g, 0.0)
                picks[x].append(pick.astype(BF16))

            def run_expert(x=x):
                for s in range(n_sub):
                    onehot = (picks[x][s] > 0.0).astype(BF16)
                    xe_ref[x, s * cap:(s + 1) * cap, :] = _dot(onehot, h_ref[s * sub:(s + 1) * sub, :]).astype(BF16)
                ye_ref[x] = _swiglu(xe_ref[x], w1_ref[x], w3_ref[x], w2_ref[x]).astype(BF16)

            if skip_done:
                pl.when(it < need[x])(run_expert)
            else:
                run_expert()
        for s in range(n_sub):
            sl = slice(s * sub, (s + 1) * sub)
            pick_all = jnp.concatenate([picks[x][s] for x in range(group)], axis=0)
            ye_all = jnp.concatenate([ye_ref[x, s * cap:(s + 1) * cap, :] for x in range(group)], axis=0)
            o_ref[sl, :] += _dot(pick_all, ye_all, TN)

    one_pass(0, False)
    n_pass = need[0]
    for x in range(1, group):
        n_pass = jnp.maximum(n_pass, need[x])

    def later_pass(it, carry):
        one_pass(it, True)
        return carry

    lax.fori_loop(1, n_pass, later_pass, 0)

    @pl.when(j == n_grp - 1)
    def _():
        o_ref[...] = x_ref[...] + mod_ref[0][5:6] * o_ref[...]


def moe(x2, mod, mod_row, h, gates, pos, w1, w3, w2, s1, s3, s2, sub, tm):
    nt, d = x2.shape
    n_exp, _, ff = w1.shape
    cap = min(MOE_CAP, sub)
    group = LANES // cap
    n_sub = tm // sub
    cnt = (pos + (gates > 0.0))[:, sub - 1::sub].reshape(n_exp, nt // tm, n_sub).max(axis=-1)
    npass = ((cnt.astype(jnp.int32) + cap - 1) // cap).T.reshape(-1)
    full = lambda a: pl.BlockSpec(a.shape, lambda i, j, n: (0,) * a.ndim)
    grid_spec = pltpu.PrefetchScalarGridSpec(
        num_scalar_prefetch=1,
        grid=(nt // tm, n_exp // group),
        in_specs=[pl.BlockSpec((tm, d), lambda i, j, n: (i, 0)),
                  pl.BlockSpec((1, 6, d), lambda i, j, n: (mod_row(i, tm), 0, 0)),
                  pl.BlockSpec((tm, d), lambda i, j, n: (i, 0)),
                  pl.BlockSpec((n_exp, tm), lambda i, j, n: (0, i)),
                  pl.BlockSpec((n_exp, tm), lambda i, j, n: (0, i)),
                  pl.BlockSpec((group, d, ff), lambda i, j, n: (j, 0, 0)),
                  pl.BlockSpec((group, d, ff), lambda i, j, n: (j, 0, 0)),
                  pl.BlockSpec((group, ff, d), lambda i, j, n: (j, 0, 0)),
                  full(s1), full(s3), full(s2)],
        out_specs=pl.BlockSpec((tm, d), lambda i, j, n: (i, 0)),
        scratch_shapes=[pltpu.VMEM((group, n_sub * cap, d), BF16), pltpu.VMEM((group, n_sub * cap, d), BF16)])
    return pl.pallas_call(
        functools.partial(_moe_kernel, sub=sub, cap=cap, group=group),
        grid_spec=grid_spec,
        out_shape=jax.ShapeDtypeStruct((nt, d), F32),
        compiler_params=_cparams(("parallel", "arbitrary")),
    )(npass, x2, mod, h, gates, pos, w1, w3, w2, s1, s3, s2)


def _final_kernel(x_ref, g_ref, o_ref):
    x = x_ref[...]
    o_ref[...] = x * lax.rsqrt(jnp.mean(x * x, axis=-1, keepdims=True) + EPS) * g_ref[...]


def final_norm(x2, g, tm=512):
    nt, d = x2.shape
    tm = min(tm, nt)
    return pl.pallas_call(
        _final_kernel,
        grid=(nt // tm,),
        in_specs=[pl.BlockSpec((tm, d), lambda i: (i, 0)), pl.BlockSpec((1, d), lambda i: (0, 0))],
        out_specs=pl.BlockSpec((tm, d), lambda i: (i, 0)),
        out_shape=jax.ShapeDtypeStruct((nt, d), F32),
        compiler_params=_cparams(("parallel",)),
    )(x2, g.reshape(1, d))


def _pad_cols(a, n):
    return jnp.pad(a, [(0, 0)] * (a.ndim - 1) + [(0, n - a.shape[-1])])


def _round_up(n, m):
    return -(-n // m) * m


def _layer(x, mod, latent, s_gla_c, s_rwkv, p, dims, grid_w):
    b, t, d = x.shape
    ctx_row = mod.shape[0] - 1
    brow = (lambda i: i) if latent else (lambda i: ctx_row)
    trow = (lambda i, tm: (i * tm) // t) if latent else (lambda i, tm: ctx_row)
    u_gla, u_rwkv = proj_in(x, mod, brow, p['norm1'], p['w_in'], dims['gla_cols'])
    hh, ww = (t // grid_w, grid_w) if latent else (b, t)
    u_rwkv = conv(u_rwkv, p['rwkv_conv'], hh, ww, vertical=latent)
    o_f, o_b, sg = gla(u_gla, s_gla_c, p['gla_w_dec'], p['gla_b_dec'], dims['gla_heads'], dims['gla_dk'], dims['gla_dv'])
    yf, yb, bonus, gate, sr = rwkv(u_rwkv, s_rwkv, p['rwkv_w2'], p['rwkv_w0'], p['rwkv_a2'], p['rwkv_a0'],
                                   p['rwkv_g2'], p['rwkv_k_k'], p['rwkv_k_a'], p['rwkv_r_k'],
                                   dims['rwkv_heads'], dims['rwkv_n'])
    x = proj_out(x, mod, brow, o_f, o_b, u_gla, p['gla_norm'], yf, yb, bonus, gate, p['rwkv_ln_g'], p['rwkv_ln_b'],
                 p['w_out'], dims['gla_heads'], dims['rwkv_n'])
    x2 = x.reshape(b * t, d)
    sub = min(MOE_SUB, t if latent else b * t)
    tm = min(MOE_SUPER, t if latent else b * t)
    h, gates, pos = route(x2, mod, trow, p['norm2'], p['router_w'], p['router_b'], sub)
    x2 = moe(x2, mod, trow, h, gates, pos, p['exp_w1'], p['exp_w3'], p['exp_w2'], p['sh_w1'], p['sh_w3'], p['sh_w2'],
             sub, tm)
    return x2.reshape(b, t, d), sg, sr


def kernel(x_prompt, x_sample, c, state_gla, state_rwkv, c_ctx, ada_w, ada_b, norm1, norm2, norm_f, w_in, w_out,
           gla_w_dec, gla_b_dec, gla_norm, rwkv_conv, rwkv_w2, rwkv_w0, rwkv_a2, rwkv_a0, rwkv_g2, rwkv_k_k, rwkv_k_a,
           rwkv_r_k, rwkv_ln_g, rwkv_ln_b, router_w, router_b, exp_w1, exp_w3, exp_w2, sh_w1, sh_w3, sh_w2):
    depth, d, _ = ada_w.shape
    bp = x_prompt.shape[0]
    bs = x_sample.shape[0]
    grid_w = 64
    gla_heads, gla_dk, gla_dv = state_gla.shape[3:]
    rwkv_heads, rwkv_n = state_rwkv.shape[3:5]
    gla_qk, gla_w, rwkv_w = gla_heads * gla_dk, gla_heads * gla_dv, rwkv_heads * rwkv_n
    gla_cols_raw = 2 * gla_qk + 2 * gla_w + 2 * gla_w_dec.shape[2]
    rwkv_cols_raw = w_in.shape[2] - gla_cols_raw
    gla_cols = 2 * gla_qk + 2 * gla_w + LANES
    rwkv_cols = _round_up(rwkv_cols_raw, LANES)
    dims = dict(gla_cols=gla_cols, gla_heads=gla_heads, gla_dk=gla_dk, gla_dv=gla_dv,
                rwkv_heads=rwkv_heads, rwkv_n=rwkv_n)

    w_in_p = jnp.concatenate([_pad_cols(w_in[..., :gla_cols_raw], gla_cols),
                              _pad_cols(w_in[..., gla_cols_raw:], rwkv_cols)], axis=-1).astype(BF16)
    conv_p = _pad_cols(rwkv_conv, rwkv_cols)
    w_out_b = w_out.astype(BF16)
    e1, e3, e2 = exp_w1.astype(BF16), exp_w3.astype(BF16), exp_w2.astype(BF16)
    s1, s3, s2 = sh_w1.astype(BF16), sh_w3.astype(BF16), sh_w2.astype(BF16)

    rows = _round_up(bs + 1, 8)
    cvec = jnp.concatenate([c, c_ctx[None], jnp.zeros((rows - bs - 1, d), F32)], axis=0)
    mod_all = adaln(cvec, ada_w, ada_b).reshape(depth, rows, 6, d)[:, :bs + 1]

    zero_gla = jnp.zeros((bp, 2, gla_dv, gla_qk), F32)
    zero_rwkv = jnp.zeros((bp, 2, rwkv_heads, rwkv_n, rwkv_n), F32)
    state_gla_c = _gla_state_in(state_gla)

    xp, xs = x_prompt, x_sample
    new_gla, new_rwkv = [], []
    for l in range(depth):
        p = dict(norm1=norm1[l], norm2=norm2[l], w_in=w_in_p[l], w_out=w_out_b[l], gla_w_dec=gla_w_dec[l],
                 gla_b_dec=gla_b_dec[l], gla_norm=gla_norm[l], rwkv_conv=conv_p[l], rwkv_w2=rwkv_w2[l],
                 rwkv_w0=rwkv_w0[l], rwkv_a2=rwkv_a2[l], rwkv_a0=rwkv_a0[l], rwkv_g2=rwkv_g2[l],
                 rwkv_k_k=rwkv_k_k[l], rwkv_k_a=rwkv_k_a[l], rwkv_r_k=rwkv_r_k[l], rwkv_ln_g=rwkv_ln_g[l],
                 rwkv_ln_b=rwkv_ln_b[l], router_w=router_w[l], router_b=router_b[l], exp_w1=e1[l], exp_w3=e3[l],
                 exp_w2=e2[l], sh_w1=s1[l], sh_w3=s3[l], sh_w2=s2[l])
        mod = mod_all[l]
        xp, sg, sr = _layer(xp, mod, False, zero_gla, zero_rwkv, p, dims, grid_w)
        new_gla.append(_gla_state_out(sg, gla_heads))
        new_rwkv.append(sr)
        xs, _, _ = _layer(xs, mod, True, state_gla_c[:, l], state_rwkv[:, l], p, dims, grid_w)
    y_prompt = final_norm(xp.reshape(-1, d), norm_f).reshape(xp.shape)
    y_sample = final_norm(xs.reshape(-1, d), norm_f).reshape(xs.shape)
    return (y_prompt, y_sample, jnp.stack(new_gla, axis=1), jnp.stack(new_rwkv, axis=1))
```

```python
import functools

import numpy as np
import jax
import jax.numpy as jnp
from jax import lax
from jax.experimental import pallas as pl
from jax.experimental.pallas import tpu as pltpu

F32 = jnp.float32
BF16 = jnp.bfloat16
HI = lax.Precision.HIGHEST

LANES = 128
EPS = 1e-6
RWKV_LN_EPS = 64e-5
GLA_NORMALIZER = 16.0
ROUTED_SCALE = 2.5
N_GROUPS = 8
TOPK_GROUPS = 4
TOP_K = 8
CHUNK = 64
VMEM_LIMIT = 56 * 1024 * 1024


def _cparams(sem):
    return pltpu.CompilerParams(dimension_semantics=sem, vmem_limit_bytes=VMEM_LIMIT)


def _dot(a, b, dims=(((1,), (0,)), ((), ())), precision=None):
    return lax.dot_general(a, b, dims, precision=precision, preferred_element_type=F32)


def _bdot(a, b, dims=(((1,), (0,)), ((), ()))):
    return lax.dot_general(a.astype(BF16), b.astype(BF16), dims, preferred_element_type=F32)


NT = (((1,), (1,)), ((), ()))
TN = (((0,), (0,)), ((), ()))
BNN = (((2,), (1,)), ((0,), (0,)))
BNT = (((2,), (2,)), ((0,), (0,)))


def _bmm(a, b, dims=BNN):
    return lax.dot_general(a.astype(BF16), b.astype(BF16), dims, preferred_element_type=F32)


def _two_terms(x):
    hi = x.astype(BF16)
    return hi, (x - hi.astype(F32)).astype(BF16)


def _split_dot(x, m):
    hi, lo = _two_terms(x)
    mb = m.astype(BF16)
    return _dot(hi, mb) + _dot(lo, mb)


def _split_dot_l(m, x):
    hi, lo = _two_terms(x)
    mb = m.astype(BF16)
    return _dot(mb, hi) + _dot(mb, lo)


def _sigmoid(x):
    return 1.0 / (1.0 + jnp.exp(-x))


def _silu(x):
    return x * _sigmoid(x)


def _softplus(x):
    return jnp.maximum(x, 0.0) + jnp.log(1.0 + jnp.exp(-jnp.abs(x)))


def _log_sigmoid(x):
    return -_softplus(-x)


def _rms_mod(x, g, scale, shift):
    y = x * lax.rsqrt(jnp.mean(x * x, axis=-1, keepdims=True) + EPS)
    return (y * g) * (1.0 + scale) + shift


def _adaln_kernel(c_ref, w_ref, b_ref, o_ref):
    c = c_ref[...]
    o_ref[0] = _dot(_silu(c), w_ref[0], precision=HI) + b_ref[0]


def adaln(cvec, ada_w, ada_b, tn=512):
    depth, d, n = ada_w.shape
    r = cvec.shape[0]
    return pl.pallas_call(
        _adaln_kernel,
        grid=(depth, n // tn),
        in_specs=[pl.BlockSpec((r, d), lambda l, j: (0, 0)),
                  pl.BlockSpec((1, d, tn), lambda l, j: (l, 0, j)),
                  pl.BlockSpec((1, 1, tn), lambda l, j: (l, 0, j))],
        out_specs=pl.BlockSpec((1, r, tn), lambda l, j: (l, 0, j)),
        out_shape=jax.ShapeDtypeStruct((depth, r, n), F32),
        compiler_params=_cparams(("parallel", "parallel")),
    )(cvec, ada_w, ada_b.reshape(depth, 1, n))


def _proj_in_kernel(x_ref, mod_ref, g_ref, w_ref, og_ref, or_ref, *, n_gla):
    m = mod_ref[0]
    h = _rms_mod(x_ref[0], g_ref[...], m[1:2], m[0:1])
    u = _bdot(h, w_ref[...])
    og_ref[0] = u[:, :n_gla]
    or_ref[0] = u[:, n_gla:]


def proj_in(x, mod, mod_row, norm_g, w, n_gla, tm=256):
    b, t, d = x.shape
    n = w.shape[1]
    tm = min(tm, t)
    return pl.pallas_call(
        functools.partial(_proj_in_kernel, n_gla=n_gla),
        grid=(b, t // tm),
        in_specs=[pl.BlockSpec((1, tm, d), lambda i, j: (i, j, 0)),
                  pl.BlockSpec((1, 6, d), lambda i, j: (mod_row(i), 0, 0)),
                  pl.BlockSpec((1, d), lambda i, j: (0, 0)),
                  pl.BlockSpec((d, n), lambda i, j: (0, 0))],
        out_specs=[pl.BlockSpec((1, tm, n_gla), lambda i, j: (i, j, 0)),
                   pl.BlockSpec((1, tm, n - n_gla), lambda i, j: (i, j, 0))],
        out_shape=[jax.ShapeDtypeStruct((b, t, n_gla), F32),
                   jax.ShapeDtypeStruct((b, t, n - n_gla), F32)],
        compiler_params=_cparams(("parallel", "parallel")),
    )(x, mod, norm_g.reshape(1, d), w)


CONV_PAD = 8


def _conv_kernel(u_ref, w_ref, o_ref, pad_ref, *, hh, ww, vertical):
    pad_ref[...] = jnp.zeros(pad_ref.shape, F32)
    pad_ref[1:hh + 1, CONV_PAD:CONV_PAD + ww, :] = u_ref[0]
    w = w_ref[...]
    acc = jnp.zeros((hh, ww, u_ref.shape[-1]), F32)
    for dy in range(3):
        if not vertical and dy != 1:
            continue
        for dx in range(3):
            acc = acc + pad_ref[dy:dy + hh, CONV_PAD - 1 + dx:CONV_PAD - 1 + dx + ww, :] * w[dy, dx]
    o_ref[0] = acc


def conv(u, wconv, hh, ww, vertical):
    c = u.shape[-1]
    b = u.size // (hh * ww * c)
    out = pl.pallas_call(
        functools.partial(_conv_kernel, hh=hh, ww=ww, vertical=vertical),
        grid=(b, c // LANES),
        in_specs=[pl.BlockSpec((1, hh, ww, LANES), lambda i, j: (i, 0, 0, j)),
                  pl.BlockSpec((3, 3, LANES), lambda i, j: (0, 0, j))],
        out_specs=pl.BlockSpec((1, hh, ww, LANES), lambda i, j: (i, 0, 0, j)),
        out_shape=jax.ShapeDtypeStruct((b, hh, ww, c), F32),
        scratch_shapes=[pltpu.VMEM((hh + 2, ww + 2 * CONV_PAD, LANES), F32)],
        compiler_params=_cparams(("parallel", "parallel")),
    )(u.reshape(b, hh, ww, c), wconv)
    return out.reshape(u.shape)


GLA_LEVELS = (32, 16, 8, 4, 2)


def _order_consts(rev):
    c = CHUNK
    idx = np.arange(c)
    pos = (c - 1 - idx) if rev else idx
    incl = (pos[None, :] <= pos[:, None])
    strict = (pos[None, :] < pos[:, None])
    return pos, incl, strict


def _gla_consts():
    c = CHUNK
    mats, masks = [], []
    for rev in (False, True):
        pos, incl, strict = _order_consts(rev)
        later = (pos[None, :] > pos[:, None])
        rows = [incl, later]
        pins, sxs, lm = [], [], [np.eye(c, dtype=bool)]
        for s in (1,) + GLA_LEVELS:
            blk = pos // s
            same = blk[None, :] == blk[:, None]
            if s > 1:
                pins.append(same & incl)
                sxs.append(same & later)
            lm.append(((blk[:, None] % 2 == 1) & (blk[None, :] == blk[:, None] - 1)).T)
        mats.append(np.concatenate(rows + pins + sxs, axis=0).astype(np.float32))
        masks.append(np.stack(lm).astype(np.float32))
    return np.stack(mats), np.stack(masks)


def _rwkv_consts():
    c = CHUNK
    out = []
    for rev in (False, True):
        pos, incl, strict = _order_consts(rev)
        b16 = pos // 16
        b32 = pos // 32
        same16 = b16[None, :] == b16[:, None]
        same32 = b32[None, :] == b32[:, None]
        out.append(np.stack([incl, strict, strict & same16, strict & same32 & ~same16,
                             strict & ~same32, np.eye(c, dtype=bool)]).astype(np.float32))
    return np.stack(out)


def _gla_kernel(uf_ref, ub_ref, s0_ref, cst_ref, msk_ref, hm_ref, wd_ref, bd_ref, of_ref, ob_ref, sf_ref, s_scr,
                *, heads, dk, dv):
    ci = pl.program_id(1)

    @pl.when(ci == 0)
    def _():
        s_scr[...] = s0_ref[0]

    c = CHUNK
    qk = heads * dk
    nl = len(GLA_LEVELS)
    hm = hm_ref[...]

    def rows_per_head(x):
        return jnp.concatenate([x] * heads, axis=0)

    dirs = (0, 1)
    us = [uf_ref[0], ub_ref[0]]
    q = [u[:, 0:qk] * (dk ** -0.5) for u in us]
    k = [u[:, qk:2 * qk] for u in us]
    vb = [u[:, 2 * qk:2 * qk + heads * dv].astype(BF16) for u in us]
    gk = [_log_sigmoid(_dot(us[d][:, 2 * qk + 2 * heads * dv:], wd_ref[d], precision=HI) + bd_ref[d])
          * (1.0 / GLA_NORMALIZER) for d in dirs]
    e = [jnp.exp(_split_dot_l(cst_ref[d], gk[d])) for d in dirs]
    qm = [rows_per_head(q[d]) * hm for d in dirs]
    kb = [k[d].astype(BF16) for d in dirs]
    attn = [_dot(kb[d], qm[d].astype(BF16), NT) * msk_ref[d, 0] for d in dirs]
    attn = [attn[d] + _dot(kb[d], (qm[d] * rows_per_head(jnp.exp(gk[d]))).astype(BF16), NT) * msk_ref[d, 1]
            for d in dirs]
    for li in range(nl):
        for d in dirs:
            qs = qm[d] * rows_per_head(e[d][(2 + li) * c:(3 + li) * c])
            ks = k[d] * e[d][(2 + nl + li) * c:(3 + nl + li) * c]
            attn[d] = attn[d] + _dot(ks.astype(BF16), qs.astype(BF16), NT) * msk_ref[d, 2 + li]
    st = [s_scr[d] for d in dirs]
    o_inter = [_dot((qm[d] * rows_per_head(e[d][0:c])).astype(BF16), st[d].astype(BF16), NT) for d in dirs]
    o_intra = [_dot(attn[d].astype(BF16), vb[d], TN) for d in dirs]
    for d, o_ref in ((0, of_ref), (1, ob_ref)):
        o_ref[0] = jnp.concatenate(
            [o_inter[d][h * c:(h + 1) * c] + o_intra[d][h * c:(h + 1) * c, h * dv:(h + 1) * dv]
             for h in range(heads)], axis=-1)
    for d in dirs:
        etot = jnp.exp(jnp.sum(gk[d], axis=0, keepdims=True))
        vstack = jnp.concatenate([vb[d][:, h * dv:(h + 1) * dv] for h in range(heads)], axis=0)
        k_rest = (rows_per_head(k[d] * e[d][c:2 * c]) * hm).astype(BF16)
        s_scr[d] = st[d] * etot + _dot(vstack, k_rest, TN)

    @pl.when(ci == pl.num_programs(1) - 1)
    def _():
        sf_ref[0] = s_scr[...]


def gla(u_gla, s0c, w_dec, b_dec, heads, dk, dv):
    b, t, cols = u_gla.shape
    nc = t // CHUNK
    qk = heads * dk
    rank = w_dec.shape[1]
    cst, msk = _gla_consts()
    msk = np.tile(msk, (1, 1, 1, heads))
    hm = np.kron(np.eye(heads, dtype=np.float32), np.ones((CHUNK, dk), np.float32))
    wd = jnp.zeros((2, LANES, qk), F32)
    wd = wd.at[0, 0:rank].set(w_dec[0]).at[1, rank:2 * rank].set(w_dec[1])
    full = lambda a: pl.BlockSpec(a.shape, lambda i, j: (0,) * a.ndim)
    consts = [jnp.asarray(cst), jnp.asarray(msk), jnp.asarray(hm), wd, b_dec.reshape(2, 1, qk)]
    return pl.pallas_call(
        functools.partial(_gla_kernel, heads=heads, dk=dk, dv=dv),
        grid=(b, nc),
        in_specs=[pl.BlockSpec((1, CHUNK, cols), lambda i, j: (i, j, 0)),
                  pl.BlockSpec((1, CHUNK, cols), lambda i, j: (i, nc - 1 - j, 0)),
                  pl.BlockSpec((1, 2, dv, qk), lambda i, j: (i, 0, 0, 0))] + [full(a) for a in consts],
        out_specs=[pl.BlockSpec((1, CHUNK, heads * dv), lambda i, j: (i, j, 0)),
                   pl.BlockSpec((1, CHUNK, heads * dv), lambda i, j: (i, nc - 1 - j, 0)),
                   pl.BlockSpec((1, 2, dv, qk), lambda i, j: (i, 0, 0, 0))],
        out_shape=[jax.ShapeDtypeStruct((b, t, heads * dv), F32),
                   jax.ShapeDtypeStruct((b, t, heads * dv), F32),
                   jax.ShapeDtypeStruct((b, 2, dv, qk), F32)],
        scratch_shapes=[pltpu.VMEM((2, dv, qk), F32)],
        compiler_params=_cparams(("parallel", "arbitrary")),
    )(u_gla, u_gla, s0c, *consts)


def _gla_state_in(s):
    h, dk, dv = s.shape[-3:]
    return jnp.moveaxis(s, -1, -3).reshape(s.shape[:-3] + (dv, h * dk))


def _gla_state_out(s, heads):
    dv, qk = s.shape[-2:]
    return jnp.moveaxis(s.reshape(s.shape[:-2] + (dv, heads, qk // heads)), -3, -1)


def _dot3(x, w_ref):
    hi, lo = _two_terms(x)
    return _dot(hi, w_ref[0]) + _dot(lo, w_ref[0]) + _dot(hi, w_ref[1])


def _rwkv_kernel(uf_ref, ub_ref, s0_ref, msk_ref, w2_ref, w0_ref, a2_ref, a0_ref, g2_ref, kk_ref, ka_ref, rk_ref,
                 bones_ref, yf_ref, yb_ref, bonus_ref, gate_ref, sf_ref, s_scr, *, heads, n):
    ci = pl.program_id(1)
    g2h = 2 * heads

    @pl.when(ci == 0)
    def _():
        s_scr[...] = s0_ref[0].reshape(g2h, n, n)

    c = CHUNK
    wd = heads * n
    half = bones_ref.shape[0]
    bones = bones_ref[...]
    dirs = (0, 1)

    def seg_sum(x):
        return jnp.concatenate([_split_dot(x[:, i:i + half], bones) for i in range(0, wd, half)], axis=-1)

    def split_heads(x):
        return jnp.stack([x[:, h * n:(h + 1) * n] for h in range(heads)], axis=0)

    def masked(x, mi):
        return jnp.concatenate([jnp.where(msk_ref[d, mi] > 0, x[d * heads:(d + 1) * heads], 0.0) for d in dirs],
                               axis=0)

    us = [uf_ref[0], ub_ref[0]]
    z0 = 3 * wd
    r = [u[:, 0:wd] for u in us]
    k = [u[:, wd:2 * wd] for u in us]
    v = [u[:, 2 * wd:3 * wd] for u in us]
    wlog = [-_softplus(-(w0_ref[d] + _dot3(jnp.tanh(us[d][:, z0 + d * n:z0 + (d + 1) * n]), w2_ref.at[d]))) - 0.5
            for d in dirs]
    logw = [-jnp.exp(wlog[d]) for d in dirs]
    a = [_sigmoid(a0_ref[...] + _dot3(us[d][:, z0 + 2 * n:z0 + 3 * n], a2_ref)) for d in dirs]
    kk = [k[d] * kk_ref[...] for d in dirs]
    kk = [kk[d] * lax.rsqrt(jnp.maximum(seg_sum(kk[d] * kk[d]), 1e-24)) for d in dirs]
    km = [k[d] * (1.0 + (a[d] - 1.0) * ka_ref[...]) for d in dirs]
    bb = [kk[d] * a[d] for d in dirs]
    cum = [_split_dot_l(msk_ref[d, 0], logw[d]) for d in dirs]
    tot = [jnp.sum(logw[d], axis=0, keepdims=True) for d in dirs]
    e_neg = [jnp.exp(-cum[d]) for d in dirs]
    e_rest = [jnp.exp(tot[d] - cum[d]) for d in dirs]
    e_tot = [jnp.exp(tot[d]) for d in dirs]
    ar = jnp.concatenate([split_heads(jnp.concatenate([kk[d] * jnp.exp(cum[d] - logw[d]), r[d] * jnp.exp(cum[d])],
                                                      axis=0)) for d in dirs], axis=0)
    kb = jnp.concatenate([split_heads(jnp.concatenate([km[d] * e_neg[d], bb[d] * e_neg[d]], axis=0)) for d in dirs],
                         axis=0)
    vv = jnp.concatenate([split_heads(v[d]) for d in dirs], axis=0)
    kbh = [jnp.concatenate([km[d] * e_rest[d], bb[d] * e_rest[d]], axis=0) for d in dirs]
    bonus_ref[0] = seg_sum(r[0] * km[0] * rk_ref[...]) * v[0]
    gate_ref[0] = _dot3(_sigmoid(us[0][:, z0 + 3 * n:z0 + 3 * n + g2_ref.shape[1]]), g2_ref)

    st = s_scr[...]
    sc = _bmm(ar, jnp.concatenate([kb, st], axis=1), BNT)
    lak = masked(sc[:, :c, :c], 1)
    lab = masked(sc[:, :c, c:2 * c], 1)
    mrk = masked(sc[:, c:, :c], 0)
    mrb = masked(sc[:, c:, c:2 * c], 0)
    rhs = sc[:, :c, 2 * c:] + _bmm(lak, vv)

    ld = masked(lab, 2)
    x = msk_ref[0, 5] - ld
    p = _bmm(ld, ld)
    x = x + _bmm(x, p)
    p = _bmm(p, p)
    x = x + _bmm(x, p)
    p = _bmm(p, p)
    x = x + _bmm(x, p)
    for mi in (3, 4):
        x = x - _bmm(x, _bmm(masked(lab, mi), x))
    uu = _bmm(x, rhs)

    vu = jnp.concatenate([vv, -uu], axis=1)
    y = sc[:, c:, 2 * c:] + _bmm(jnp.concatenate([mrk, mrb], axis=2), vu)
    yf_ref[0] = jnp.concatenate([y[h] for h in range(heads)], axis=-1)
    yb_ref[0] = jnp.concatenate([y[heads + h] for h in range(heads)], axis=-1)
    for d in dirs:
        for h in range(heads):
            sl = slice(h * n, (h + 1) * n)
            g = d * heads + h
            s_scr[g] = st[g] * e_tot[d][:, sl] + _bdot(vu[g], kbh[d][:, sl], TN)

    @pl.when(ci == pl.num_programs(1) - 1)
    def _():
        sf_ref[0] = s_scr[...].reshape(2, heads, n, n)


def rwkv(u, s0, w2, w0, a2, a0, g2, k_k, k_a, r_k, heads, n):
    b, t, cols = u.shape
    nc = t // CHUNK
    wd = heads * n
    msk = _rwkv_consts()
    bones = np.kron(np.eye(heads // 2, dtype=np.float32), np.ones((n, n), np.float32))
    row = lambda x: x.reshape(1, wd)
    full = lambda a: pl.BlockSpec(a.shape, lambda i, j: (0,) * a.ndim)
    terms = lambda w: jnp.stack(_two_terms(w), axis=-3)
    args = [jnp.asarray(msk), terms(w2), w0.reshape(2, 1, wd), terms(a2), row(a0), terms(g2), row(k_k), row(k_a),
            row(r_k), jnp.asarray(bones)]
    return pl.pallas_call(
        functools.partial(_rwkv_kernel, heads=heads, n=n),
        grid=(b, nc),
        in_specs=[pl.BlockSpec((1, CHUNK, cols), lambda i, j: (i, j, 0)),
                  pl.BlockSpec((1, CHUNK, cols), lambda i, j: (i, nc - 1 - j, 0)),
                  pl.BlockSpec((1, 2, heads, n, n), lambda i, j: (i, 0, 0, 0, 0))] + [full(a) for a in args],
        out_specs=[pl.BlockSpec((1, CHUNK, wd), lambda i, j: (i, j, 0)),
                   pl.BlockSpec((1, CHUNK, wd), lambda i, j: (i, nc - 1 - j, 0)),
                   pl.BlockSpec((1, CHUNK, wd), lambda i, j: (i, j, 0)),
                   pl.BlockSpec((1, CHUNK, wd), lambda i, j: (i, j, 0)),
                   pl.BlockSpec((1, 2, heads, n, n), lambda i, j: (i, 0, 0, 0, 0))],
        out_shape=[jax.ShapeDtypeStruct((b, t, wd), F32)] * 4
        + [jax.ShapeDtypeStruct((b, 2, heads, n, n), F32)],
        scratch_shapes=[pltpu.VMEM((2 * heads, n, n), F32)],
        compiler_params=_cparams(("parallel", "arbitrary")),
    )(u, u, s0, *args)


def _proj_out_kernel(x_ref, mod_ref, of_ref, ob_ref, gg_ref, gn_ref, yf_ref, yb_ref, bonus_ref, gate_ref,
                     lng_ref, lnb_ref, bmean_ref, w_ref, o_ref, *, gla_heads):
    m = mod_ref[0]
    o = of_ref[0] + ob_ref[0]
    gw = o.shape[-1]
    dv = gw // gla_heads
    parts = []
    for h in range(gla_heads):
        oh = o[:, h * dv:(h + 1) * dv]
        parts.append(oh * lax.rsqrt(jnp.mean(oh * oh, axis=-1, keepdims=True) + EPS))
    o_gla = jnp.concatenate(parts, axis=-1) * gn_ref[...] * _silu(gg_ref[0])
    y = yf_ref[0] + yb_ref[0]
    bmean = bmean_ref[...]
    dy = y - _split_dot(y, bmean)
    var = _split_dot(dy * dy, bmean)
    yn = dy * lax.rsqrt(var + RWKV_LN_EPS) * lng_ref[...] + lnb_ref[...]
    o_rwkv = (yn + bonus_ref[0]) * gate_ref[0]
    w = w_ref[...]
    mix = _bdot(o_gla, w[:gw]) + _bdot(o_rwkv, w[gw:])
    o_ref[0] = x_ref[0] + m[2:3] * mix


def proj_out(x, mod, mod_row, o_f, o_b, u_gla, gla_norm, yf, yb, bonus, gate, ln_g, ln_b, w_out, gla_heads, rwkv_n,
             tm=256):
    b, t, d = x.shape
    gw = o_f.shape[-1]
    rw = yf.shape[-1]
    tm = min(tm, t)
    bmean = np.kron(np.eye(rw // rwkv_n, dtype=np.float32), np.full((rwkv_n, rwkv_n), 1.0 / rwkv_n, np.float32))
    tile = lambda wdt: pl.BlockSpec((1, tm, wdt), lambda i, j: (i, j, 0))
    row = lambda wdt: pl.BlockSpec((1, wdt), lambda i, j: (0, 0))
    g_col_block = (u_gla.shape[-1] - LANES - gw) // gw
    return pl.pallas_call(
        functools.partial(_proj_out_kernel, gla_heads=gla_heads),
        grid=(b, t // tm),
        in_specs=[tile(d),
                  pl.BlockSpec((1, 6, d), lambda i, j: (mod_row(i), 0, 0)),
                  tile(gw), tile(gw),
                  pl.BlockSpec((1, tm, gw), lambda i, j: (i, j, g_col_block)),
                  row(gw), tile(rw), tile(rw), tile(rw), tile(rw), row(rw), row(rw),
                  pl.BlockSpec((rw, rw), lambda i, j: (0, 0)),
                  pl.BlockSpec(w_out.shape, lambda i, j: (0, 0))],
        out_specs=tile(d),
        out_shape=jax.ShapeDtypeStruct((b, t, d), F32),
        compiler_params=_cparams(("parallel", "parallel")),
    )(x, mod, o_f, o_b, u_gla, gla_norm.reshape(1, gw), yf, yb, bonus, gate, ln_g.reshape(1, rw),
      ln_b.reshape(1, rw), jnp.asarray(bmean), w_out)


ROUTE_TILE = 256
MOE_SUB = 128
MOE_CAP = 32
MOE_SUPER = 1024


def _route_kernel(x_ref, mod_ref, g_ref, rw_ref, rb_ref, tri_ref, h_ref, gates_ref, pos_ref, *, n_exp):
    m = mod_ref[0]
    h = _rms_mod(x_ref[...], g_ref[...], m[4:5], m[3:4])
    h_ref[...] = h.astype(BF16)
    tm = h.shape[0]
    gsz = n_exp // N_GROUPS
    scores = _sigmoid(_dot(rw_ref[...], h, NT, precision=HI))
    sel = scores + rb_ref[...]
    neg = jnp.float32(-jnp.inf)
    s3 = sel.reshape(N_GROUPS, gsz, tm)
    idx = lax.broadcasted_iota(jnp.int32, s3.shape, 1)
    m1 = jnp.max(s3, axis=1, keepdims=True)
    first = jnp.min(jnp.where(s3 == m1, idx, gsz), axis=1, keepdims=True)
    m2 = jnp.max(jnp.where(idx == first, neg, s3), axis=1, keepdims=True)
    gs = (m1 + m2).reshape(N_GROUPS, tm)
    gidx = lax.broadcasted_iota(jnp.int32, gs.shape, 0)
    cnt = jnp.zeros(gs.shape, jnp.int32)
    for g in range(N_GROUPS):
        other = gs[g:g + 1]
        cnt = cnt + ((other > gs) | ((other == gs) & (g < gidx))).astype(jnp.int32)
    gkeep = jnp.broadcast_to((cnt < TOPK_GROUPS)[:, None, :], s3.shape).reshape(n_exp, tm)
    selm = jnp.where(gkeep, sel, neg)
    eidx = lax.broadcasted_iota(jnp.int32, selm.shape, 0)
    rank = jnp.zeros(selm.shape, jnp.int32)
    for e in range(n_exp):
        other = selm[e:e + 1]
        rank = rank + ((other > selm) | ((other == selm) & (e < eidx))).astype(jnp.int32)
    chosen = rank < TOP_K
    wts = jnp.where(chosen, scores, 0.0)
    gates_ref[...] = wts / jnp.sum(wts, axis=0, keepdims=True) * ROUTED_SCALE
    pos_ref[...] = _dot(chosen.astype(BF16), tri_ref[...])


def route(x2, mod, mod_row, norm_g, router_w, router_b, tm, sub):
    nt, d = x2.shape
    n_exp = router_w.shape[1]
    tri = np.triu(np.ones((tm, tm), np.float32), 1) * np.kron(np.eye(tm // sub), np.ones((sub, sub)))
    return pl.pallas_call(
        functools.partial(_route_kernel, n_exp=n_exp),
        grid=(nt // tm,),
        in_specs=[pl.BlockSpec((tm, d), lambda i: (i, 0)),
                  pl.BlockSpec((1, 6, d), lambda i: (mod_row(i, tm), 0, 0)),
                  pl.BlockSpec((1, d), lambda i: (0, 0)),
                  pl.BlockSpec((n_exp, d), lambda i: (0, 0)),
                  pl.BlockSpec((n_exp, 1), lambda i: (0, 0)),
                  pl.BlockSpec((tm, tm), lambda i: (0, 0))],
        out_specs=[pl.BlockSpec((tm, d), lambda i: (i, 0)),
                   pl.BlockSpec((n_exp, tm), lambda i: (0, i)),
                   pl.BlockSpec((n_exp, tm), lambda i: (0, i))],
        out_shape=[jax.ShapeDtypeStruct((nt, d), BF16),
                   jax.ShapeDtypeStruct((n_exp, nt), F32),
                   jax.ShapeDtypeStruct((n_exp, nt), F32)],
        compiler_params=_cparams(("parallel",)),
    )(x2, mod, norm_g.reshape(1, d), router_w.T, router_b.reshape(n_exp, 1), jnp.asarray(tri, BF16))


def _swiglu(h, w1, w3, w2):
    return _bdot(_silu(_bdot(h, w1)) * _bdot(h, w3), w2)


def _moe_kernel(npass_ref, x_ref, mod_ref, h_ref, gates_ref, pos_ref, w1_ref, w3_ref, w2_ref, s1_ref, s3_ref, s2_ref,
                o_ref, xe_ref, ye_ref, *, sub, cap, group):
    i = pl.program_id(0)
    j = pl.program_id(1)
    n_grp = pl.num_programs(1)
    n_sub = h_ref.shape[0] // sub

    @pl.when(j == 0)
    def _():
        o_ref[...] = _swiglu(h_ref[...], s1_ref[...], s3_ref[...], s2_ref[...])

    first = (i * n_grp + j) * group
    gate = [gates_ref[pl.ds(j * group + x, 1), :] for x in range(group)]
    pos = [pos_ref[pl.ds(j * group + x, 1), :] for x in range(group)]
    need = [npass_ref[first + x] for x in range(group)]
    rows = lax.broadcasted_iota(jnp.int32, (cap, sub), 0).astype(F32)

    def one_pass(it, skip_done):
        base = it * float(cap) if isinstance(it, int) else (it * cap).astype(F32)
        picks = []
        for x in range(group):
            picks.append([])
            for s in range(n_sub):
                sl = slice(s * sub, (s + 1) * sub)
                g = gate[x][:, sl]
                pick = jnp.where((pos[x][:, sl] - base == rows) & (g > 0.0), g, 0.0)
                picks[x].append(pick.astype(BF16))

            def run_expert(x=x):
                for s in range(n_sub):
                    onehot = (picks[x][s] > 0.0).astype(BF16)
                    xe_ref[x, s * cap:(s + 1) * cap, :] = _dot(onehot, h_ref[s * sub:(s + 1) * sub, :]).astype(BF16)
                ye_ref[x] = _swiglu(xe_ref[x], w1_ref[x], w3_ref[x], w2_ref[x]).astype(BF16)

            if skip_done:
                pl.when(it < need[x])(run_expert)
            else:
                run_expert()
        for s in range(n_sub):
            sl = slice(s * sub, (s + 1) * sub)
            pick_all = jnp.concatenate([picks[x][s] for x in range(group)], axis=0)
            ye_all = jnp.concatenate([ye_ref[x, s * cap:(s + 1) * cap, :] for x in range(group)], axis=0)
            o_ref[sl, :] += _dot(pick_all, ye_all, TN)

    one_pass(0, False)
    n_pass = need[0]
    for x in range(1, group):
        n_pass = jnp.maximum(n_pass, need[x])

    def later_pass(it, carry):
        one_pass(it, True)
        return carry

    lax.fori_loop(1, n_pass, later_pass, 0)

    @pl.when(j == n_grp - 1)
    def _():
        o_ref[...] = x_ref[...] + mod_ref[0][5:6] * o_ref[...]


def moe(x2, mod, mod_row, h, gates, pos, w1, w3, w2, s1, s3, s2, sub, tm):
    nt, d = x2.shape
    n_exp, _, ff = w1.shape
    cap = min(MOE_CAP, sub)
    group = LANES // cap
    n_sub = tm // sub
    cnt = (pos + (gates > 0.0))[:, sub - 1::sub].reshape(n_exp, nt // tm, n_sub).max(axis=-1)
    npass = ((cnt.astype(jnp.int32) + cap - 1) // cap).T.reshape(-1)
    full = lambda a: pl.BlockSpec(a.shape, lambda i, j, n: (0,) * a.ndim)
    grid_spec = pltpu.PrefetchScalarGridSpec(
        num_scalar_prefetch=1,
        grid=(nt // tm, n_exp // group),
        in_specs=[pl.BlockSpec((tm, d), lambda i, j, n: (i, 0)),
                  pl.BlockSpec((1, 6, d), lambda i, j, n: (mod_row(i, tm), 0, 0)),
                  pl.BlockSpec((tm, d), lambda i, j, n: (i, 0)),
                  pl.BlockSpec((n_exp, tm), lambda i, j, n: (0, i)),
                  pl.BlockSpec((n_exp, tm), lambda i, j, n: (0, i)),
                  pl.BlockSpec((group, d, ff), lambda i, j, n: (j, 0, 0)),
                  pl.BlockSpec((group, d, ff), lambda i, j, n: (j, 0, 0)),
                  pl.BlockSpec((group, ff, d), lambda i, j, n: (j, 0, 0)),
                  full(s1), full(s3), full(s2)],
        out_specs=pl.BlockSpec((tm, d), lambda i, j, n: (i, 0)),
        scratch_shapes=[pltpu.VMEM((group, n_sub * cap, d), BF16), pltpu.VMEM((group, n_sub * cap, d), BF16)])
    return pl.pallas_call(
        functools.partial(_moe_kernel, sub=sub, cap=cap, group=group),
        grid_spec=grid_spec,
        out_shape=jax.ShapeDtypeStruct((nt, d), F32),
        compiler_params=_cparams(("parallel", "arbitrary")),
    )(npass, x2, mod, h, gates, pos, w1, w3, w2, s1, s3, s2)


def _final_kernel(x_ref, g_ref, o_ref):
    x = x_ref[...]
    o_ref[...] = x * lax.rsqrt(jnp.mean(x * x, axis=-1, keepdims=True) + EPS) * g_ref[...]


def final_norm(x2, g, tm=512):
    nt, d = x2.shape
    tm = min(tm, nt)
    return pl.pallas_call(
        _final_kernel,
        grid=(nt // tm,),
        in_specs=[pl.BlockSpec((tm, d), lambda i: (i, 0)), pl.BlockSpec((1, d), lambda i: (0, 0))],
        out_specs=pl.BlockSpec((tm, d), lambda i: (i, 0)),
        out_shape=jax.ShapeDtypeStruct((nt, d), F32),
        compiler_params=_cparams(("parallel",)),
    )(x2, g.reshape(1, d))


def _pad_cols(a, n):
    return jnp.pad(a, [(0, 0)] * (a.ndim - 1) + [(0, n - a.shape[-1])])


def _round_up(n, m):
    return -(-n // m) * m


def _layer(x, mod, latent, s_gla_c, s_rwkv, p, dims, grid_w):
    b, t, d = x.shape
    ctx_row = mod.shape[0] - 1
    brow = (lambda i: i) if latent else (lambda i: ctx_row)
    trow = (lambda i, tm: (i * tm) // t) if latent else (lambda i, tm: ctx_row)
    u_gla, u_rwkv = proj_in(x, mod, brow, p['norm1'], p['w_in'], dims['gla_cols'])
    hh, ww = (t // grid_w, grid_w) if latent else (b, t)
    u_rwkv = conv(u_rwkv, p['rwkv_conv'], hh, ww, vertical=latent)
    o_f, o_b, sg = gla(u_gla, s_gla_c, p['gla_w_dec'], p['gla_b_dec'], dims['gla_heads'], dims['gla_dk'], dims['gla_dv'])
    yf, yb, bonus, gate, sr = rwkv(u_rwkv, s_rwkv, p['rwkv_w2'], p['rwkv_w0'], p['rwkv_a2'], p['rwkv_a0'],
                                   p['rwkv_g2'], p['rwkv_k_k'], p['rwkv_k_a'], p['rwkv_r_k'],
                                   dims['rwkv_heads'], dims['rwkv_n'])
    x = proj_out(x, mod, brow, o_f, o_b, u_gla, p['gla_norm'], yf, yb, bonus, gate, p['rwkv_ln_g'], p['rwkv_ln_b'],
                 p['w_out'], dims['gla_heads'], dims['rwkv_n'])
    x2 = x.reshape(b * t, d)
    sub = min(MOE_SUB, t if latent else b * t)
    tm = min(MOE_SUPER, t if latent else b * t)
    h, gates, pos = route(x2, mod, trow, p['norm2'], p['router_w'], p['router_b'], min(ROUTE_TILE, tm), sub)
    x2 = moe(x2, mod, trow, h, gates, pos, p['exp_w1'], p['exp_w3'], p['exp_w2'], p['sh_w1'], p['sh_w3'], p['sh_w2'],
             sub, tm)
    return x2.reshape(b, t, d), sg, sr


def kernel(x_prompt, x_sample, c, state_gla, state_rwkv, c_ctx, ada_w, ada_b, norm1, norm2, norm_f, w_in, w_out,
           gla_w_dec, gla_b_dec, gla_norm, rwkv_conv, rwkv_w2, rwkv_w0, rwkv_a2, rwkv_a0, rwkv_g2, rwkv_k_k, rwkv_k_a,
           rwkv_r_k, rwkv_ln_g, rwkv_ln_b, router_w, router_b, exp_w1, exp_w3, exp_w2, sh_w1, sh_w3, sh_w2):
    depth, d, _ = ada_w.shape
    bp = x_prompt.shape[0]
    bs = x_sample.shape[0]
    grid_w = 64
    gla_heads, gla_dk, gla_dv = state_gla.shape[3:]
    rwkv_heads, rwkv_n = state_rwkv.shape[3:5]
    gla_qk, gla_w, rwkv_w = gla_heads * gla_dk, gla_heads * gla_dv, rwkv_heads * rwkv_n
    gla_cols_raw = 2 * gla_qk + 2 * gla_w + 2 * gla_w_dec.shape[2]
    rwkv_cols_raw = w_in.shape[2] - gla_cols_raw
    gla_cols = 2 * gla_qk + 2 * gla_w + LANES
    rwkv_cols = _round_up(rwkv_cols_raw, LANES)
    dims = dict(gla_cols=gla_cols, gla_heads=gla_heads, gla_dk=gla_dk, gla_dv=gla_dv,
                rwkv_heads=rwkv_heads, rwkv_n=rwkv_n)

    w_in_p = jnp.concatenate([_pad_cols(w_in[..., :gla_cols_raw], gla_cols),
                              _pad_cols(w_in[..., gla_cols_raw:], rwkv_cols)], axis=-1).astype(BF16)
    conv_p = _pad_cols(rwkv_conv, rwkv_cols)
    w_out_b = w_out.astype(BF16)
    e1, e3, e2 = exp_w1.astype(BF16), exp_w3.astype(BF16), exp_w2.astype(BF16)
    s1, s3, s2 = sh_w1.astype(BF16), sh_w3.astype(BF16), sh_w2.astype(BF16)

    rows = _round_up(bs + 1, 8)
    cvec = jnp.concatenate([c, c_ctx[None], jnp.zeros((rows - bs - 1, d), F32)], axis=0)
    mod_all = adaln(cvec, ada_w, ada_b).reshape(depth, rows, 6, d)[:, :bs + 1]

    zero_gla = jnp.zeros((bp, 2, gla_dv, gla_qk), F32)
    zero_rwkv = jnp.zeros((bp, 2, rwkv_heads, rwkv_n, rwkv_n), F32)
    state_gla_c = _gla_state_in(state_gla)

    xp, xs = x_prompt, x_sample
    new_gla, new_rwkv = [], []
    for l in range(depth):
        p = dict(norm1=norm1[l], norm2=norm2[l], w_in=w_in_p[l], w_out=w_out_b[l], gla_w_dec=gla_w_dec[l],
                 gla_b_dec=gla_b_dec[l], gla_norm=gla_norm[l], rwkv_conv=conv_p[l], rwkv_w2=rwkv_w2[l],
                 rwkv_w0=rwkv_w0[l], rwkv_a2=rwkv_a2[l], rwkv_a0=rwkv_a0[l], rwkv_g2=rwkv_g2[l],
                 rwkv_k_k=rwkv_k_k[l], rwkv_k_a=rwkv_k_a[l], rwkv_r_k=rwkv_r_k[l], rwkv_ln_g=rwkv_ln_g[l],
                 rwkv_ln_b=rwkv_ln_b[l], router_w=router_w[l], router_b=router_b[l], exp_w1=e1[l], exp_w3=e3[l],
                 exp_w2=e2[l], sh_w1=s1[l], sh_w3=s3[l], sh_w2=s2[l])
        mod = mod_all[l]
        xp, sg, sr = _layer(xp, mod, False, zero_gla, zero_rwkv, p, dims, grid_w)
        new_gla.append(_gla_state_out(sg, gla_heads))
        new_rwkv.append(sr)
        xs, _, _ = _layer(xs, mod, True, state_gla_c[:, l], state_rwkv[:, l], p, dims, grid_w)
    y_prompt = final_norm(xp.reshape(-1, d), norm_f).reshape(xp.shape)
    y_sample = final_norm(xs.reshape(-1, d), norm_f).reshape(xs.shape)
    return (y_prompt, y_sample, jnp.stack(new_gla, axis=1), jnp.stack(new_rwkv, axis=1))
```

```python
import functools

import numpy as np
import jax
import jax.numpy as jnp
from jax import lax
from jax.experimental import pallas as pl
from jax.experimental.pallas import tpu as pltpu

F32 = jnp.float32
BF16 = jnp.bfloat16
HI = lax.Precision.HIGHEST

LANES = 128
EPS = 1e-6
RWKV_LN_EPS = 64e-5
GLA_NORMALIZER = 16.0
ROUTED_SCALE = 2.5
N_GROUPS = 8
TOPK_GROUPS = 4
TOP_K = 8
CHUNK = 64
VMEM_LIMIT = 56 * 1024 * 1024


def _cparams(sem):
    return pltpu.CompilerParams(dimension_semantics=sem, vmem_limit_bytes=VMEM_LIMIT)


def _dot(a, b, dims=(((1,), (0,)), ((), ())), precision=None):
    return lax.dot_general(a, b, dims, precision=precision, preferred_element_type=F32)


def _bdot(a, b, dims=(((1,), (0,)), ((), ()))):
    return lax.dot_general(a.astype(BF16), b.astype(BF16), dims, preferred_element_type=F32)


NT = (((1,), (1,)), ((), ()))
TN = (((0,), (0,)), ((), ()))
BNN = (((2,), (1,)), ((0,), (0,)))
BNT = (((2,), (2,)), ((0,), (0,)))


def _bmm(a, b, dims=BNN):
    return lax.dot_general(a.astype(BF16), b.astype(BF16), dims, preferred_element_type=F32)


def _two_terms(x):
    hi = x.astype(BF16)
    return hi, (x - hi.astype(F32)).astype(BF16)


def _split_dot(x, m):
    hi, lo = _two_terms(x)
    mb = m.astype(BF16)
    return _dot(hi, mb) + _dot(lo, mb)


def _split_dot_l(m, x):
    hi, lo = _two_terms(x)
    mb = m.astype(BF16)
    return _dot(mb, hi) + _dot(mb, lo)


def _sigmoid(x):
    return 1.0 / (1.0 + jnp.exp(-x))


def _silu(x):
    return x * _sigmoid(x)


def _softplus(x):
    return jnp.maximum(x, 0.0) + jnp.log(1.0 + jnp.exp(-jnp.abs(x)))


def _log_sigmoid(x):
    return -_softplus(-x)


def _rms_mod(x, g, scale, shift):
    y = x * lax.rsqrt(jnp.mean(x * x, axis=-1, keepdims=True) + EPS)
    return (y * g) * (1.0 + scale) + shift


def _adaln_kernel(c_ref, w_ref, b_ref, o_ref):
    c = c_ref[...]
    o_ref[0] = _dot(_silu(c), w_ref[0], precision=HI) + b_ref[0]


def adaln(cvec, ada_w, ada_b, tn=512):
    depth, d, n = ada_w.shape
    r = cvec.shape[0]
    return pl.pallas_call(
        _adaln_kernel,
        grid=(depth, n // tn),
        in_specs=[pl.BlockSpec((r, d), lambda l, j: (0, 0)),
                  pl.BlockSpec((1, d, tn), lambda l, j: (l, 0, j)),
                  pl.BlockSpec((1, 1, tn), lambda l, j: (l, 0, j))],
        out_specs=pl.BlockSpec((1, r, tn), lambda l, j: (l, 0, j)),
        out_shape=jax.ShapeDtypeStruct((depth, r, n), F32),
        compiler_params=_cparams(("parallel", "parallel")),
    )(cvec, ada_w, ada_b.reshape(depth, 1, n))


def _proj_in_kernel(x_ref, mod_ref, g_ref, w_ref, og_ref, or_ref, *, n_gla):
    m = mod_ref[0]
    h = _rms_mod(x_ref[0], g_ref[...], m[1:2], m[0:1])
    u = _bdot(h, w_ref[...])
    og_ref[0] = u[:, :n_gla]
    or_ref[0] = u[:, n_gla:]


def proj_in(x, mod, mod_row, norm_g, w, n_gla, tm=256):
    b, t, d = x.shape
    n = w.shape[1]
    tm = min(tm, t)
    return pl.pallas_call(
        functools.partial(_proj_in_kernel, n_gla=n_gla),
        grid=(b, t // tm),
        in_specs=[pl.BlockSpec((1, tm, d), lambda i, j: (i, j, 0)),
                  pl.BlockSpec((1, 6, d), lambda i, j: (mod_row(i), 0, 0)),
                  pl.BlockSpec((1, d), lambda i, j: (0, 0)),
                  pl.BlockSpec((d, n), lambda i, j: (0, 0))],
        out_specs=[pl.BlockSpec((1, tm, n_gla), lambda i, j: (i, j, 0)),
                   pl.BlockSpec((1, tm, n - n_gla), lambda i, j: (i, j, 0))],
        out_shape=[jax.ShapeDtypeStruct((b, t, n_gla), F32),
                   jax.ShapeDtypeStruct((b, t, n - n_gla), F32)],
        compiler_params=_cparams(("parallel", "parallel")),
    )(x, mod, norm_g.reshape(1, d), w)


CONV_PAD = 8


def _conv_kernel(u_ref, w_ref, o_ref, pad_ref, *, hh, ww, vertical):
    pad_ref[...] = jnp.zeros(pad_ref.shape, F32)
    pad_ref[1:hh + 1, CONV_PAD:CONV_PAD + ww, :] = u_ref[0]
    w = w_ref[...]
    acc = jnp.zeros((hh, ww, u_ref.shape[-1]), F32)
    for dy in range(3):
        if not vertical and dy != 1:
            continue
        for dx in range(3):
            acc = acc + pad_ref[dy:dy + hh, CONV_PAD - 1 + dx:CONV_PAD - 1 + dx + ww, :] * w[dy, dx]
    o_ref[0] = acc


def conv(u, wconv, hh, ww, vertical):
    c = u.shape[-1]
    b = u.size // (hh * ww * c)
    out = pl.pallas_call(
        functools.partial(_conv_kernel, hh=hh, ww=ww, vertical=vertical),
        grid=(b, c // LANES),
        in_specs=[pl.BlockSpec((1, hh, ww, LANES), lambda i, j: (i, 0, 0, j)),
                  pl.BlockSpec((3, 3, LANES), lambda i, j: (0, 0, j))],
        out_specs=pl.BlockSpec((1, hh, ww, LANES), lambda i, j: (i, 0, 0, j)),
        out_shape=jax.ShapeDtypeStruct((b, hh, ww, c), F32),
        scratch_shapes=[pltpu.VMEM((hh + 2, ww + 2 * CONV_PAD, LANES), F32)],
        compiler_params=_cparams(("parallel", "parallel")),
    )(u.reshape(b, hh, ww, c), wconv)
    return out.reshape(u.shape)


GLA_LEVELS = (32, 16, 8, 4, 2)
GLA_STEP_CHUNKS = 2


def _order_consts(rev):
    c = CHUNK
    idx = np.arange(c)
    pos = (c - 1 - idx) if rev else idx
    incl = (pos[None, :] <= pos[:, None])
    strict = (pos[None, :] < pos[:, None])
    return pos, incl, strict


def _gla_consts():
    c = CHUNK
    mats, masks = [], []
    for rev in (False, True):
        pos, incl, strict = _order_consts(rev)
        later = (pos[None, :] > pos[:, None])
        rows = [incl, later]
        pins, sxs, lm = [], [], [np.eye(c, dtype=bool)]
        for s in (1,) + GLA_LEVELS:
            blk = pos // s
            same = blk[None, :] == blk[:, None]
            if s > 1:
                pins.append(same & incl)
                sxs.append(same & later)
            lm.append(((blk[:, None] % 2 == 1) & (blk[None, :] == blk[:, None] - 1)).T)
        mats.append(np.concatenate(rows + pins + sxs, axis=0).astype(np.float32))
        masks.append(np.stack(lm).astype(np.float32))
    return np.stack(mats), np.stack(masks)


def _rwkv_consts():
    c = CHUNK
    out = []
    for rev in (False, True):
        pos, incl, strict = _order_consts(rev)
        b16 = pos // 16
        b32 = pos // 32
        same16 = b16[None, :] == b16[:, None]
        same32 = b32[None, :] == b32[:, None]
        out.append(np.stack([incl, strict, strict & same16, strict & same32 & ~same16,
                             strict & ~same32, np.eye(c, dtype=bool)]).astype(np.float32))
    return np.stack(out)


def _gla_kernel(uf_ref, ub_ref, s0_ref, cst_ref, msk_ref, hm_ref, wd_ref, bd_ref, of_ref, ob_ref, sf_ref, s_scr,
                *, heads, dk, dv):
    ci = pl.program_id(1)

    @pl.when(ci == 0)
    def _():
        s_scr[...] = s0_ref[0]

    c = CHUNK
    qk = heads * dk
    nl = len(GLA_LEVELS)
    hm = hm_ref[...]

    def rows_per_head(x):
        return jnp.concatenate([x] * heads, axis=0)

    dirs = (0, 1)
    chains = [(d, (o if d == 0 else GLA_STEP_CHUNKS - 1 - o)) for o in range(GLA_STEP_CHUNKS) for d in dirs]
    nch = range(len(chains))
    us = [(uf_ref, ub_ref)[d][0, rb * c:(rb + 1) * c, :] for d, rb in chains]
    q = [u[:, 0:qk] * (dk ** -0.5) for u in us]
    k = [u[:, qk:2 * qk] for u in us]
    vb = [u[:, 2 * qk:2 * qk + heads * dv].astype(BF16) for u in us]
    gk = [_log_sigmoid(_dot(us[i][:, 2 * qk + 2 * heads * dv:], wd_ref[chains[i][0]], precision=HI)
                       + bd_ref[chains[i][0]]) * (1.0 / GLA_NORMALIZER) for i in nch]
    e = [jnp.exp(_split_dot_l(cst_ref[chains[i][0]], gk[i])) for i in nch]
    qm = [rows_per_head(q[i]) * hm for i in nch]
    kb = [k[i].astype(BF16) for i in nch]
    attn = [_dot(kb[i], qm[i].astype(BF16), NT) * msk_ref[chains[i][0], 0] for i in nch]
    attn = [attn[i] + _dot(kb[i], (qm[i] * rows_per_head(jnp.exp(gk[i]))).astype(BF16), NT) * msk_ref[chains[i][0], 1]
            for i in nch]
    for li in range(nl):
        for i in nch:
            qs = qm[i] * rows_per_head(e[i][(2 + li) * c:(3 + li) * c])
            ks = k[i] * e[i][(2 + nl + li) * c:(3 + nl + li) * c]
            attn[i] = attn[i] + _dot(ks.astype(BF16), qs.astype(BF16), NT) * msk_ref[chains[i][0], 2 + li]
    o_intra = [_dot(attn[i].astype(BF16), vb[i], TN) for i in nch]
    q_cum = [(qm[i] * rows_per_head(e[i][0:c])).astype(BF16) for i in nch]
    k_rest = [(rows_per_head(k[i] * e[i][c:2 * c]) * hm).astype(BF16) for i in nch]
    etot = [jnp.exp(jnp.sum(gk[i], axis=0, keepdims=True)) for i in nch]
    st = [s_scr[d] for d in dirs]
    for i in nch:
        d, rb = chains[i]
        o_inter = _dot(q_cum[i], st[d].astype(BF16), NT)
        (of_ref, ob_ref)[d][0, rb * c:(rb + 1) * c, :] = jnp.concatenate(
            [o_inter[h * c:(h + 1) * c] + o_intra[i][h * c:(h + 1) * c, h * dv:(h + 1) * dv] for h in range(heads)],
            axis=-1)
        vstack = jnp.concatenate([vb[i][:, h * dv:(h + 1) * dv] for h in range(heads)], axis=0)
        st[d] = st[d] * etot[i] + _dot(vstack, k_rest[i], TN)
    for d in dirs:
        s_scr[d] = st[d]

    @pl.when(ci == pl.num_programs(1) - 1)
    def _():
        sf_ref[0] = s_scr[...]


def gla(u_gla, s0c, w_dec, b_dec, heads, dk, dv):
    b, t, cols = u_gla.shape
    blk = GLA_STEP_CHUNKS * CHUNK
    nc = t // blk
    qk = heads * dk
    rank = w_dec.shape[1]
    cst, msk = _gla_consts()
    msk = np.tile(msk, (1, 1, 1, heads))
    hm = np.kron(np.eye(heads, dtype=np.float32), np.ones((CHUNK, dk), np.float32))
    wd = jnp.zeros((2, LANES, qk), F32)
    wd = wd.at[0, 0:rank].set(w_dec[0]).at[1, rank:2 * rank].set(w_dec[1])
    full = lambda a: pl.BlockSpec(a.shape, lambda i, j: (0,) * a.ndim)
    consts = [jnp.asarray(cst), jnp.asarray(msk), jnp.asarray(hm), wd, b_dec.reshape(2, 1, qk)]
    return pl.pallas_call(
        functools.partial(_gla_kernel, heads=heads, dk=dk, dv=dv),
        grid=(b, nc),
        in_specs=[pl.BlockSpec((1, blk, cols), lambda i, j: (i, j, 0)),
                  pl.BlockSpec((1, blk, cols), lambda i, j: (i, nc - 1 - j, 0)),
                  pl.BlockSpec((1, 2, dv, qk), lambda i, j: (i, 0, 0, 0))] + [full(a) for a in consts],
        out_specs=[pl.BlockSpec((1, blk, heads * dv), lambda i, j: (i, j, 0)),
                   pl.BlockSpec((1, blk, heads * dv), lambda i, j: (i, nc - 1 - j, 0)),
                   pl.BlockSpec((1, 2, dv, qk), lambda i, j: (i, 0, 0, 0))],
        out_shape=[jax.ShapeDtypeStruct((b, t, heads * dv), F32),
                   jax.ShapeDtypeStruct((b, t, heads * dv), F32),
                   jax.ShapeDtypeStruct((b, 2, dv, qk), F32)],
        scratch_shapes=[pltpu.VMEM((2, dv, qk), F32)],
        compiler_params=_cparams(("parallel", "arbitrary")),
    )(u_gla, u_gla, s0c, *consts)


def _gla_state_in(s):
    h, dk, dv = s.shape[-3:]
    return jnp.moveaxis(s, -1, -3).reshape(s.shape[:-3] + (dv, h * dk))


def _gla_state_out(s, heads):
    dv, qk = s.shape[-2:]
    return jnp.moveaxis(s.reshape(s.shape[:-2] + (dv, heads, qk // heads)), -3, -1)


def _dot3(x, w_ref):
    hi, lo = _two_terms(x)
    return _dot(hi, w_ref[0]) + _dot(lo, w_ref[0]) + _dot(hi, w_ref[1])


def _rwkv_kernel(uf_ref, ub_ref, s0_ref, msk_ref, w2_ref, w0_ref, a2_ref, a0_ref, g2_ref, kk_ref, ka_ref, rk_ref,
                 bones_ref, yf_ref, yb_ref, bonus_ref, gate_ref, sf_ref, s_scr, *, heads, n):
    ci = pl.program_id(1)
    g2h = 2 * heads

    @pl.when(ci == 0)
    def _():
        s_scr[...] = s0_ref[0].reshape(g2h, n, n)

    c = CHUNK
    wd = heads * n
    half = bones_ref.shape[0]
    bones = bones_ref[...]
    dirs = (0, 1)

    def seg_sum(x):
        return jnp.concatenate([_split_dot(x[:, i:i + half], bones) for i in range(0, wd, half)], axis=-1)

    def split_heads(x):
        return jnp.stack([x[:, h * n:(h + 1) * n] for h in range(heads)], axis=0)

    def masked(x, mi):
        return jnp.concatenate([jnp.where(msk_ref[d, mi] > 0, x[d * heads:(d + 1) * heads], 0.0) for d in dirs],
                               axis=0)

    us = [uf_ref[0], ub_ref[0]]
    z0 = 3 * wd
    r = [u[:, 0:wd] for u in us]
    k = [u[:, wd:2 * wd] for u in us]
    v = [u[:, 2 * wd:3 * wd] for u in us]
    wlog = [-_softplus(-(w0_ref[d] + _dot3(jnp.tanh(us[d][:, z0 + d * n:z0 + (d + 1) * n]), w2_ref.at[d]))) - 0.5
            for d in dirs]
    logw = [-jnp.exp(wlog[d]) for d in dirs]
    a = [_sigmoid(a0_ref[...] + _dot3(us[d][:, z0 + 2 * n:z0 + 3 * n], a2_ref)) for d in dirs]
    kk = [k[d] * kk_ref[...] for d in dirs]
    kk = [kk[d] * lax.rsqrt(jnp.maximum(seg_sum(kk[d] * kk[d]), 1e-24)) for d in dirs]
    km = [k[d] * (1.0 + (a[d] - 1.0) * ka_ref[...]) for d in dirs]
    bb = [kk[d] * a[d] for d in dirs]
    cum = [_split_dot_l(msk_ref[d, 0], logw[d]) for d in dirs]
    tot = [jnp.sum(logw[d], axis=0, keepdims=True) for d in dirs]
    e_neg = [jnp.exp(-cum[d]) for d in dirs]
    e_rest = [jnp.exp(tot[d] - cum[d]) for d in dirs]
    e_tot = [jnp.exp(tot[d]) for d in dirs]
    ar = jnp.concatenate([split_heads(jnp.concatenate([kk[d] * jnp.exp(cum[d] - logw[d]), r[d] * jnp.exp(cum[d])],
                                                      axis=0)) for d in dirs], axis=0)
    kb = jnp.concatenate([split_heads(jnp.concatenate([km[d] * e_neg[d], bb[d] * e_neg[d]], axis=0)) for d in dirs],
                         axis=0)
    vv = jnp.concatenate([split_heads(v[d]) for d in dirs], axis=0)
    kbh = [jnp.concatenate([km[d] * e_rest[d], bb[d] * e_rest[d]], axis=0) for d in dirs]
    bonus_ref[0] = seg_sum(r[0] * km[0] * rk_ref[...]) * v[0]
    gate_ref[0] = _dot3(_sigmoid(us[0][:, z0 + 3 * n:z0 + 3 * n + g2_ref.shape[1]]), g2_ref)

    st = s_scr[...]
    sc = _bmm(ar, jnp.concatenate([kb, st], axis=1), BNT)
    lak = masked(sc[:, :c, :c], 1)
    lab = masked(sc[:, :c, c:2 * c], 1)
    mrk = masked(sc[:, c:, :c], 0)
    mrb = masked(sc[:, c:, c:2 * c], 0)
    rhs = sc[:, :c, 2 * c:] + _bmm(lak, vv)

    ld = masked(lab, 2)
    x = msk_ref[0, 5] - ld
    p = _bmm(ld, ld)
    x = x + _bmm(x, p)
    p = _bmm(p, p)
    x = x + _bmm(x, p)
    p = _bmm(p, p)
    x = x + _bmm(x, p)
    for mi in (3, 4):
        x = x - _bmm(x, _bmm(masked(lab, mi), x))
    uu = _bmm(x, rhs)

    vu = jnp.concatenate([vv, -uu], axis=1)
    y = sc[:, c:, 2 * c:] + _bmm(jnp.concatenate([mrk, mrb], axis=2), vu)
    yf_ref[0] = jnp.concatenate([y[h] for h in range(heads)], axis=-1)
    yb_ref[0] = jnp.concatenate([y[heads + h] for h in range(heads)], axis=-1)
    for d in dirs:
        for h in range(heads):
            sl = slice(h * n, (h + 1) * n)
            g = d * heads + h
            s_scr[g] = st[g] * e_tot[d][:, sl] + _bdot(vu[g], kbh[d][:, sl], TN)

    @pl.when(ci == pl.num_programs(1) - 1)
    def _():
        sf_ref[0] = s_scr[...].reshape(2, heads, n, n)


def rwkv(u, s0, w2, w0, a2, a0, g2, k_k, k_a, r_k, heads, n):
    b, t, cols = u.shape
    nc = t // CHUNK
    wd = heads * n
    msk = _rwkv_consts()
    bones = np.kron(np.eye(heads // 2, dtype=np.float32), np.ones((n, n), np.float32))
    row = lambda x: x.reshape(1, wd)
    full = lambda a: pl.BlockSpec(a.shape, lambda i, j: (0,) * a.ndim)
    terms = lambda w: jnp.stack(_two_terms(w), axis=-3)
    args = [jnp.asarray(msk), terms(w2), w0.reshape(2, 1, wd), terms(a2), row(a0), terms(g2), row(k_k), row(k_a),
            row(r_k), jnp.asarray(bones)]
    return pl.pallas_call(
        functools.partial(_rwkv_kernel, heads=heads, n=n),
        grid=(b, nc),
        in_specs=[pl.BlockSpec((1, CHUNK, cols), lambda i, j: (i, j, 0)),
                  pl.BlockSpec((1, CHUNK, cols), lambda i, j: (i, nc - 1 - j, 0)),
                  pl.BlockSpec((1, 2, heads, n, n), lambda i, j: (i, 0, 0, 0, 0))] + [full(a) for a in args],
        out_specs=[pl.BlockSpec((1, CHUNK, wd), lambda i, j: (i, j, 0)),
                   pl.BlockSpec((1, CHUNK, wd), lambda i, j: (i, nc - 1 - j, 0)),
                   pl.BlockSpec((1, CHUNK, wd), lambda i, j: (i, j, 0)),
                   pl.BlockSpec((1, CHUNK, wd), lambda i, j: (i, j, 0)),
                   pl.BlockSpec((1, 2, heads, n, n), lambda i, j: (i, 0, 0, 0, 0))],
        out_shape=[jax.ShapeDtypeStruct((b, t, wd), F32)] * 4
        + [jax.ShapeDtypeStruct((b, 2, heads, n, n), F32)],
        scratch_shapes=[pltpu.VMEM((2 * heads, n, n), F32)],
        compiler_params=_cparams(("parallel", "arbitrary")),
    )(u, u, s0, *args)


def _proj_out_kernel(x_ref, mod_ref, of_ref, ob_ref, gg_ref, gn_ref, yf_ref, yb_ref, bonus_ref, gate_ref,
                     lng_ref, lnb_ref, bmean_ref, w_ref, o_ref, *, gla_heads):
    m = mod_ref[0]
    o = of_ref[0] + ob_ref[0]
    gw = o.shape[-1]
    dv = gw // gla_heads
    parts = []
    for h in range(gla_heads):
        oh = o[:, h * dv:(h + 1) * dv]
        parts.append(oh * lax.rsqrt(jnp.mean(oh * oh, axis=-1, keepdims=True) + EPS))
    o_gla = jnp.concatenate(parts, axis=-1) * gn_ref[...] * _silu(gg_ref[0])
    y = yf_ref[0] + yb_ref[0]
    bmean = bmean_ref[...]
    dy = y - _split_dot(y, bmean)
    var = _split_dot(dy * dy, bmean)
    yn = dy * lax.rsqrt(var + RWKV_LN_EPS) * lng_ref[...] + lnb_ref[...]
    o_rwkv = (yn + bonus_ref[0]) * gate_ref[0]
    w = w_ref[...]
    mix = _bdot(o_gla, w[:gw]) + _bdot(o_rwkv, w[gw:])
    o_ref[0] = x_ref[0] + m[2:3] * mix


def proj_out(x, mod, mod_row, o_f, o_b, u_gla, gla_norm, yf, yb, bonus, gate, ln_g, ln_b, w_out, gla_heads, rwkv_n,
             tm=256):
    b, t, d = x.shape
    gw = o_f.shape[-1]
    rw = yf.shape[-1]
    tm = min(tm, t)
    bmean = np.kron(np.eye(rw // rwkv_n, dtype=np.float32), np.full((rwkv_n, rwkv_n), 1.0 / rwkv_n, np.float32))
    tile = lambda wdt: pl.BlockSpec((1, tm, wdt), lambda i, j: (i, j, 0))
    row = lambda wdt: pl.BlockSpec((1, wdt), lambda i, j: (0, 0))
    g_col_block = (u_gla.shape[-1] - LANES - gw) // gw
    return pl.pallas_call(
        functools.partial(_proj_out_kernel, gla_heads=gla_heads),
        grid=(b, t // tm),
        in_specs=[tile(d),
                  pl.BlockSpec((1, 6, d), lambda i, j: (mod_row(i), 0, 0)),
                  tile(gw), tile(gw),
                  pl.BlockSpec((1, tm, gw), lambda i, j: (i, j, g_col_block)),
                  row(gw), tile(rw), tile(rw), tile(rw), tile(rw), row(rw), row(rw),
                  pl.BlockSpec((rw, rw), lambda i, j: (0, 0)),
                  pl.BlockSpec(w_out.shape, lambda i, j: (0, 0))],
        out_specs=tile(d),
        out_shape=jax.ShapeDtypeStruct((b, t, d), F32),
        compiler_params=_cparams(("parallel", "parallel")),
    )(x, mod, o_f, o_b, u_gla, gla_norm.reshape(1, gw), yf, yb, bonus, gate, ln_g.reshape(1, rw),
      ln_b.reshape(1, rw), jnp.asarray(bmean), w_out)


ROUTE_TILE = 256
MOE_SUB = 128
MOE_CAP = 32
MOE_SUPER = 1024


def _route_kernel(x_ref, mod_ref, g_ref, rw_ref, rb_ref, tri_ref, h_ref, gates_ref, pos_ref, *, n_exp):
    m = mod_ref[0]
    h = _rms_mod(x_ref[...], g_ref[...], m[4:5], m[3:4])
    h_ref[...] = h.astype(BF16)
    tm = h.shape[0]
    gsz = n_exp // N_GROUPS
    scores = _sigmoid(_dot(rw_ref[...], h, NT, precision=HI))
    sel = scores + rb_ref[...]
    neg = jnp.float32(-jnp.inf)
    s3 = sel.reshape(N_GROUPS, gsz, tm)
    idx = lax.broadcasted_iota(jnp.int32, s3.shape, 1)
    m1 = jnp.max(s3, axis=1, keepdims=True)
    first = jnp.min(jnp.where(s3 == m1, idx, gsz), axis=1, keepdims=True)
    m2 = jnp.max(jnp.where(idx == first, neg, s3), axis=1, keepdims=True)
    gs = (m1 + m2).reshape(N_GROUPS, tm)
    gidx = lax.broadcasted_iota(jnp.int32, gs.shape, 0)
    cnt = jnp.zeros(gs.shape, jnp.int32)
    for g in range(N_GROUPS):
        other = gs[g:g + 1]
        cnt = cnt + ((other > gs) | ((other == gs) & (g < gidx))).astype(jnp.int32)
    gkeep = jnp.broadcast_to((cnt < TOPK_GROUPS)[:, None, :], s3.shape).reshape(n_exp, tm)
    selm = jnp.where(gkeep, sel, neg)
    eidx = lax.broadcasted_iota(jnp.int32, selm.shape, 0)
    rank = jnp.zeros(selm.shape, jnp.int32)
    for e in range(n_exp):
        other = selm[e:e + 1]
        rank = rank + ((other > selm) | ((other == selm) & (e < eidx))).astype(jnp.int32)
    chosen = rank < TOP_K
    wts = jnp.where(chosen, scores, 0.0)
    gates_ref[...] = wts / jnp.sum(wts, axis=0, keepdims=True) * ROUTED_SCALE
    pos_ref[...] = _dot(chosen.astype(BF16), tri_ref[...])


def route(x2, mod, mod_row, norm_g, router_w, router_b, tm, sub):
    nt, d = x2.shape
    n_exp = router_w.shape[1]
    tri = np.triu(np.ones((tm, tm), np.float32), 1) * np.kron(np.eye(tm // sub), np.ones((sub, sub)))
    return pl.pallas_call(
        functools.partial(_route_kernel, n_exp=n_exp),
        grid=(nt // tm,),
        in_specs=[pl.BlockSpec((tm, d), lambda i: (i, 0)),
                  pl.BlockSpec((1, 6, d), lambda i: (mod_row(i, tm), 0, 0)),
                  pl.BlockSpec((1, d), lambda i: (0, 0)),
                  pl.BlockSpec((n_exp, d), lambda i: (0, 0)),
                  pl.BlockSpec((n_exp, 1), lambda i: (0, 0)),
                  pl.BlockSpec((tm, tm), lambda i: (0, 0))],
        out_specs=[pl.BlockSpec((tm, d), lambda i: (i, 0)),
                   pl.BlockSpec((n_exp, tm), lambda i: (0, i)),
                   pl.BlockSpec((n_exp, tm), lambda i: (0, i))],
        out_shape=[jax.ShapeDtypeStruct((nt, d), BF16),
                   jax.ShapeDtypeStruct((n_exp, nt), F32),
                   jax.ShapeDtypeStruct((n_exp, nt), F32)],
        compiler_params=_cparams(("parallel",)),
    )(x2, mod, norm_g.reshape(1, d), router_w.T, router_b.reshape(n_exp, 1), jnp.asarray(tri, BF16))


def _swiglu(h, w1, w3, w2):
    return _bdot(_silu(_bdot(h, w1)) * _bdot(h, w3), w2)


def _moe_kernel(npass_ref, x_ref, mod_ref, h_ref, gates_ref, pos_ref, w1_ref, w3_ref, w2_ref, s1_ref, s3_ref, s2_ref,
                o_ref, xe_ref, ye_ref, *, sub, cap, group):
    i = pl.program_id(0)
    j = pl.program_id(1)
    n_grp = pl.num_programs(1)
    n_sub = h_ref.shape[0] // sub

    @pl.when(j == 0)
    def _():
        o_ref[...] = _swiglu(h_ref[...], s1_ref[...], s3_ref[...], s2_ref[...])

    first = (i * n_grp + j) * group
    gate = [gates_ref[pl.ds(j * group + x, 1), :] for x in range(group)]
    pos = [pos_ref[pl.ds(j * group + x, 1), :] for x in range(group)]
    need = [npass_ref[first + x] for x in range(group)]
    rows = lax.broadcasted_iota(jnp.int32, (cap, sub), 0).astype(F32)

    def one_pass(it, skip_done):
        base = it * float(cap) if isinstance(it, int) else (it * cap).astype(F32)
        picks = []
        for x in range(group):
            picks.append([])
            for s in range(n_sub):
                sl = slice(s * sub, (s + 1) * sub)
                g = gate[x][:, sl]
                pick = jnp.where((pos[x][:, sl] - base == rows) & (g > 0.0), g, 0.0)
                picks[x].append(pick.astype(BF16))

            def run_expert(x=x):
                for s in range(n_sub):
                    onehot = (picks[x][s] > 0.0).astype(BF16)
                    xe_ref[x, s * cap:(s + 1) * cap, :] = _dot(onehot, h_ref[s * sub:(s + 1) * sub, :]).astype(BF16)
                ye_ref[x] = _swiglu(xe_ref[x], w1_ref[x], w3_ref[x], w2_ref[x]).astype(BF16)

            if skip_done:
                pl.when(it < need[x])(run_expert)
        if not skip_done:
            for s in range(n_sub):
                onehot = jnp.concatenate([(picks[x][s] > 0.0).astype(BF16) for x in range(group)], axis=0)
                rows_all = _dot(onehot, h_ref[s * sub:(s + 1) * sub, :]).astype(BF16)
                for x in range(group):
                    xe_ref[x, s * cap:(s + 1) * cap, :] = rows_all[x * cap:(x + 1) * cap]
            mid = [_silu(_bdot(xe_ref[x], w1_ref[x])) * _bdot(xe_ref[x], w3_ref[x]) for x in range(group)]
            for x in range(group):
                ye_ref[x] = _bdot(mid[x], w2_ref[x]).astype(BF16)
        for s in range(n_sub):
            sl = slice(s * sub, (s + 1) * sub)
            pick_all = jnp.concatenate([picks[x][s] for x in range(group)], axis=0)
            ye_all = jnp.concatenate([ye_ref[x, s * cap:(s + 1) * cap, :] for x in range(group)], axis=0)
            o_ref[sl, :] += _dot(pick_all, ye_all, TN)

    one_pass(0, False)
    n_pass = need[0]
    for x in range(1, group):
        n_pass = jnp.maximum(n_pass, need[x])

    def later_pass(it, carry):
        one_pass(it, True)
        return carry

    lax.fori_loop(1, n_pass, later_pass, 0)

    @pl.when(j == n_grp - 1)
    def _():
        o_ref[...] = x_ref[...] + mod_ref[0][5:6] * o_ref[...]


def moe(x2, mod, mod_row, h, gates, pos, w1, w3, w2, s1, s3, s2, sub, tm):
    nt, d = x2.shape
    n_exp, _, ff = w1.shape
    cap = min(MOE_CAP, sub)
    group = LANES // cap
    n_sub = tm // sub
    cnt = (pos + (gates > 0.0))[:, sub - 1::sub].reshape(n_exp, nt // tm, n_sub).max(axis=-1)
    npass = ((cnt.astype(jnp.int32) + cap - 1) // cap).T.reshape(-1)
    full = lambda a: pl.BlockSpec(a.shape, lambda i, j, n: (0,) * a.ndim)
    grid_spec = pltpu.PrefetchScalarGridSpec(
        num_scalar_prefetch=1,
        grid=(nt // tm, n_exp // group),
        in_specs=[pl.BlockSpec((tm, d), lambda i, j, n: (i, 0)),
                  pl.BlockSpec((1, 6, d), lambda i, j, n: (mod_row(i, tm), 0, 0)),
                  pl.BlockSpec((tm, d), lambda i, j, n: (i, 0)),
                  pl.BlockSpec((n_exp, tm), lambda i, j, n: (0, i)),
                  pl.BlockSpec((n_exp, tm), lambda i, j, n: (0, i)),
                  pl.BlockSpec((group, d, ff), lambda i, j, n: (j, 0, 0)),
                  pl.BlockSpec((group, d, ff), lambda i, j, n: (j, 0, 0)),
                  pl.BlockSpec((group, ff, d), lambda i, j, n: (j, 0, 0)),
                  full(s1), full(s3), full(s2)],
        out_specs=pl.BlockSpec((tm, d), lambda i, j, n: (i, 0)),
        scratch_shapes=[pltpu.VMEM((group, n_sub * cap, d), BF16), pltpu.VMEM((group, n_sub * cap, d), BF16)])
    return pl.pallas_call(
        functools.partial(_moe_kernel, sub=sub, cap=cap, group=group),
        grid_spec=grid_spec,
        out_shape=jax.ShapeDtypeStruct((nt, d), F32),
        compiler_params=_cparams(("parallel", "arbitrary")),
    )(npass, x2, mod, h, gates, pos, w1, w3, w2, s1, s3, s2)


def _final_kernel(x_ref, g_ref, o_ref):
    x = x_ref[...]
    o_ref[...] = x * lax.rsqrt(jnp.mean(x * x, axis=-1, keepdims=True) + EPS) * g_ref[...]


def final_norm(x2, g, tm=512):
    nt, d = x2.shape
    tm = min(tm, nt)
    return pl.pallas_call(
        _final_kernel,
        grid=(nt // tm,),
        in_specs=[pl.BlockSpec((tm, d), lambda i: (i, 0)), pl.BlockSpec((1, d), lambda i: (0, 0))],
        out_specs=pl.BlockSpec((tm, d), lambda i: (i, 0)),
        out_shape=jax.ShapeDtypeStruct((nt, d), F32),
        compiler_params=_cparams(("parallel",)),
    )(x2, g.reshape(1, d))


def _pad_cols(a, n):
    return jnp.pad(a, [(0, 0)] * (a.ndim - 1) + [(0, n - a.shape[-1])])


def _round_up(n, m):
    return -(-n // m) * m


def _layer(x, mod, latent, s_gla_c, s_rwkv, p, dims, grid_w):
    b, t, d = x.shape
    ctx_row = mod.shape[0] - 1
    brow = (lambda i: i) if latent else (lambda i: ctx_row)
    trow = (lambda i, tm: (i * tm) // t) if latent else (lambda i, tm: ctx_row)
    u_gla, u_rwkv = proj_in(x, mod, brow, p['norm1'], p['w_in'], dims['gla_cols'])
    hh, ww = (t // grid_w, grid_w) if latent else (b, t)
    u_rwkv = conv(u_rwkv, p['rwkv_conv'], hh, ww, vertical=latent)
    o_f, o_b, sg = gla(u_gla, s_gla_c, p['gla_w_dec'], p['gla_b_dec'], dims['gla_heads'], dims['gla_dk'], dims['gla_dv'])
    yf, yb, bonus, gate, sr = rwkv(u_rwkv, s_rwkv, p['rwkv_w2'], p['rwkv_w0'], p['rwkv_a2'], p['rwkv_a0'],
                                   p['rwkv_g2'], p['rwkv_k_k'], p['rwkv_k_a'], p['rwkv_r_k'],
                                   dims['rwkv_heads'], dims['rwkv_n'])
    x = proj_out(x, mod, brow, o_f, o_b, u_gla, p['gla_norm'], yf, yb, bonus, gate, p['rwkv_ln_g'], p['rwkv_ln_b'],
                 p['w_out'], dims['gla_heads'], dims['rwkv_n'])
    x2 = x.reshape(b * t, d)
    sub = min(MOE_SUB, t if latent else b * t)
    tm = min(MOE_SUPER, t if latent else b * t)
    h, gates, pos = route(x2, mod, trow, p['norm2'], p['router_w'], p['router_b'], min(ROUTE_TILE, tm), sub)
    x2 = moe(x2, mod, trow, h, gates, pos, p['exp_w1'], p['exp_w3'], p['exp_w2'], p['sh_w1'], p['sh_w3'], p['sh_w2'],
             sub, tm)
    return x2.reshape(b, t, d), sg, sr


def kernel(x_prompt, x_sample, c, state_gla, state_rwkv, c_ctx, ada_w, ada_b, norm1, norm2, norm_f, w_in, w_out,
           gla_w_dec, gla_b_dec, gla_norm, rwkv_conv, rwkv_w2, rwkv_w0, rwkv_a2, rwkv_a0, rwkv_g2, rwkv_k_k, rwkv_k_a,
           rwkv_r_k, rwkv_ln_g, rwkv_ln_b, router_w, router_b, exp_w1, exp_w3, exp_w2, sh_w1, sh_w3, sh_w2):
    depth, d, _ = ada_w.shape
    bp = x_prompt.shape[0]
    bs = x_sample.shape[0]
    grid_w = 64
    gla_heads, gla_dk, gla_dv = state_gla.shape[3:]
    rwkv_heads, rwkv_n = state_rwkv.shape[3:5]
    gla_qk, gla_w, rwkv_w = gla_heads * gla_dk, gla_heads * gla_dv, rwkv_heads * rwkv_n
    gla_cols_raw = 2 * gla_qk + 2 * gla_w + 2 * gla_w_dec.shape[2]
    rwkv_cols_raw = w_in.shape[2] - gla_cols_raw
    gla_cols = 2 * gla_qk + 2 * gla_w + LANES
    rwkv_cols = _round_up(rwkv_cols_raw, LANES)
    dims = dict(gla_cols=gla_cols, gla_heads=gla_heads, gla_dk=gla_dk, gla_dv=gla_dv,
                rwkv_heads=rwkv_heads, rwkv_n=rwkv_n)

    w_in_p = jnp.concatenate([_pad_cols(w_in[..., :gla_cols_raw], gla_cols),
                              _pad_cols(w_in[..., gla_cols_raw:], rwkv_cols)], axis=-1).astype(BF16)
    conv_p = _pad_cols(rwkv_conv, rwkv_cols)
    w_out_b = w_out.astype(BF16)
    e1, e3, e2 = exp_w1.astype(BF16), exp_w3.astype(BF16), exp_w2.astype(BF16)
    s1, s3, s2 = sh_w1.astype(BF16), sh_w3.astype(BF16), sh_w2.astype(BF16)

    rows = _round_up(bs + 1, 8)
    cvec = jnp.concatenate([c, c_ctx[None], jnp.zeros((rows - bs - 1, d), F32)], axis=0)
    mod_all = adaln(cvec, ada_w, ada_b).reshape(depth, rows, 6, d)[:, :bs + 1]

    zero_gla = jnp.zeros((bp, 2, gla_dv, gla_qk), F32)
    zero_rwkv = jnp.zeros((bp, 2, rwkv_heads, rwkv_n, rwkv_n), F32)
    state_gla_c = _gla_state_in(state_gla)

    xp, xs = x_prompt, x_sample
    new_gla, new_rwkv = [], []
    for l in range(depth):
        p = dict(norm1=norm1[l], norm2=norm2[l], w_in=w_in_p[l], w_out=w_out_b[l], gla_w_dec=gla_w_dec[l],
                 gla_b_dec=gla_b_dec[l], gla_norm=gla_norm[l], rwkv_conv=conv_p[l], rwkv_w2=rwkv_w2[l],
                 rwkv_w0=rwkv_w0[l], rwkv_a2=rwkv_a2[l], rwkv_a0=rwkv_a0[l], rwkv_g2=rwkv_g2[l],
                 rwkv_k_k=rwkv_k_k[l], rwkv_k_a=rwkv_k_a[l], rwkv_r_k=rwkv_r_k[l], rwkv_ln_g=rwkv_ln_g[l],
                 rwkv_ln_b=rwkv_ln_b[l], router_w=router_w[l], router_b=router_b[l], exp_w1=e1[l], exp_w3=e3[l],
                 exp_w2=e2[l], sh_w1=s1[l], sh_w3=s3[l], sh_w2=s2[l])
        mod = mod_all[l]
        xp, sg, sr = _layer(xp, mod, False, zero_gla, zero_rwkv, p, dims, grid_w)
        new_gla.append(_gla_state_out(sg, gla_heads))
        new_rwkv.append(sr)
        xs, _, _ = _layer(xs, mod, True, state_gla_c[:, l], state_rwkv[:, l], p, dims, grid_w)
    y_prompt = final_norm(xp.reshape(-1, d), norm_f).reshape(xp.shape)
    y_sample = final_norm(xs.reshape(-1, d), norm_f).reshape(xs.shape)
    return (y_prompt, y_sample, jnp.stack(new_gla, axis=1), jnp.stack(new_rwkv, axis=1))
```

```python
import functools

import numpy as np
import jax
import jax.numpy as jnp
from jax import lax
from jax.experimental import pallas as pl
from jax.experimental.pallas import tpu as pltpu

F32 = jnp.float32
BF16 = jnp.bfloat16
HI = lax.Precision.HIGHEST

LANES = 128
EPS = 1e-6
RWKV_LN_EPS = 64e-5
GLA_NORMALIZER = 16.0
ROUTED_SCALE = 2.5
N_GROUPS = 8
TOPK_GROUPS = 4
TOP_K = 8
CHUNK = 64
RWKV_STEP_CHUNKS = 2
VMEM_LIMIT = 56 * 1024 * 1024


def _cparams(sem):
    return pltpu.CompilerParams(dimension_semantics=sem, vmem_limit_bytes=VMEM_LIMIT)


def _dot(a, b, dims=(((1,), (0,)), ((), ())), precision=None):
    return lax.dot_general(a, b, dims, precision=precision, preferred_element_type=F32)


def _bdot(a, b, dims=(((1,), (0,)), ((), ()))):
    return lax.dot_general(a.astype(BF16), b.astype(BF16), dims, preferred_element_type=F32)


NT = (((1,), (1,)), ((), ()))
TN = (((0,), (0,)), ((), ()))
BNN = (((2,), (1,)), ((0,), (0,)))
BNT = (((2,), (2,)), ((0,), (0,)))


def _bmm(a, b, dims=BNN):
    return lax.dot_general(a.astype(BF16), b.astype(BF16), dims, preferred_element_type=F32)


def _two_terms(x):
    hi = x.astype(BF16)
    return hi, (x - hi.astype(F32)).astype(BF16)


def _split_dot(x, m):
    hi, lo = _two_terms(x)
    mb = m.astype(BF16)
    return _dot(hi, mb) + _dot(lo, mb)


def _split_dot_l(m, x):
    hi, lo = _two_terms(x)
    mb = m.astype(BF16)
    return _dot(mb, hi) + _dot(mb, lo)


def _sigmoid(x):
    return 1.0 / (1.0 + jnp.exp(-x))


def _silu(x):
    return x * _sigmoid(x)


def _softplus(x):
    return jnp.maximum(x, 0.0) + jnp.log(1.0 + jnp.exp(-jnp.abs(x)))


def _log_sigmoid(x):
    return -_softplus(-x)


def _rms_mod(x, g, scale, shift):
    y = x * lax.rsqrt(jnp.mean(x * x, axis=-1, keepdims=True) + EPS)
    return (y * g) * (1.0 + scale) + shift


def _adaln_kernel(c_ref, w_ref, b_ref, o_ref):
    c = c_ref[...]
    o_ref[0] = _dot(_silu(c), w_ref[0], precision=HI) + b_ref[0]


def adaln(cvec, ada_w, ada_b, tn=512):
    depth, d, n = ada_w.shape
    r = cvec.shape[0]
    return pl.pallas_call(
        _adaln_kernel,
        grid=(depth, n // tn),
        in_specs=[pl.BlockSpec((r, d), lambda l, j: (0, 0)),
                  pl.BlockSpec((1, d, tn), lambda l, j: (l, 0, j)),
                  pl.BlockSpec((1, 1, tn), lambda l, j: (l, 0, j))],
        out_specs=pl.BlockSpec((1, r, tn), lambda l, j: (l, 0, j)),
        out_shape=jax.ShapeDtypeStruct((depth, r, n), F32),
        compiler_params=_cparams(("parallel", "parallel")),
    )(cvec, ada_w, ada_b.reshape(depth, 1, n))


def _proj_in_kernel(x_ref, mod_ref, g_ref, w_ref, og_ref, or_ref, *, n_gla):
    m = mod_ref[0]
    h = _rms_mod(x_ref[0], g_ref[...], m[1:2], m[0:1])
    u = _bdot(h, w_ref[...])
    og_ref[0] = u[:, :n_gla]
    or_ref[0] = u[:, n_gla:]


def proj_in(x, mod, mod_row, norm_g, w, n_gla, tm=256):
    b, t, d = x.shape
    n = w.shape[1]
    tm = min(tm, t)
    return pl.pallas_call(
        functools.partial(_proj_in_kernel, n_gla=n_gla),
        grid=(b, t // tm),
        in_specs=[pl.BlockSpec((1, tm, d), lambda i, j: (i, j, 0)),
                  pl.BlockSpec((1, 6, d), lambda i, j: (mod_row(i), 0, 0)),
                  pl.BlockSpec((1, d), lambda i, j: (0, 0)),
                  pl.BlockSpec((d, n), lambda i, j: (0, 0))],
        out_specs=[pl.BlockSpec((1, tm, n_gla), lambda i, j: (i, j, 0)),
                   pl.BlockSpec((1, tm, n - n_gla), lambda i, j: (i, j, 0))],
        out_shape=[jax.ShapeDtypeStruct((b, t, n_gla), F32),
                   jax.ShapeDtypeStruct((b, t, n - n_gla), F32)],
        compiler_params=_cparams(("parallel", "parallel")),
    )(x, mod, norm_g.reshape(1, d), w)


CONV_PAD = 8


def _conv_kernel(u_ref, w_ref, o_ref, pad_ref, *, hh, ww, vertical):
    pad_ref[...] = jnp.zeros(pad_ref.shape, F32)
    pad_ref[1:hh + 1, CONV_PAD:CONV_PAD + ww, :] = u_ref[0]
    w = w_ref[...]
    acc = jnp.zeros((hh, ww, u_ref.shape[-1]), F32)
    for dy in range(3):
        if not vertical and dy != 1:
            continue
        for dx in range(3):
            acc = acc + pad_ref[dy:dy + hh, CONV_PAD - 1 + dx:CONV_PAD - 1 + dx + ww, :] * w[dy, dx]
    o_ref[0] = acc


def conv(u, wconv, hh, ww, vertical):
    c = u.shape[-1]
    b = u.size // (hh * ww * c)
    out = pl.pallas_call(
        functools.partial(_conv_kernel, hh=hh, ww=ww, vertical=vertical),
        grid=(b, c // LANES),
        in_specs=[pl.BlockSpec((1, hh, ww, LANES), lambda i, j: (i, 0, 0, j)),
                  pl.BlockSpec((3, 3, LANES), lambda i, j: (0, 0, j))],
        out_specs=pl.BlockSpec((1, hh, ww, LANES), lambda i, j: (i, 0, 0, j)),
        out_shape=jax.ShapeDtypeStruct((b, hh, ww, c), F32),
        scratch_shapes=[pltpu.VMEM((hh + 2, ww + 2 * CONV_PAD, LANES), F32)],
        compiler_params=_cparams(("parallel", "parallel")),
    )(u.reshape(b, hh, ww, c), wconv)
    return out.reshape(u.shape)


GLA_LEVELS = (32, 16, 8, 4, 2)
GLA_STEP_CHUNKS = 4


def _order_consts(rev):
    c = CHUNK
    idx = np.arange(c)
    pos = (c - 1 - idx) if rev else idx
    incl = (pos[None, :] <= pos[:, None])
    strict = (pos[None, :] < pos[:, None])
    return pos, incl, strict


def _gla_consts():
    c = CHUNK
    mats, masks = [], []
    for rev in (False, True):
        pos, incl, strict = _order_consts(rev)
        later = (pos[None, :] > pos[:, None])
        rows = [incl, later]
        pins, sxs, lm = [], [], [np.eye(c, dtype=bool)]
        for s in (1,) + GLA_LEVELS:
            blk = pos // s
            same = blk[None, :] == blk[:, None]
            if s > 1:
                pins.append(same & incl)
                sxs.append(same & later)
            lm.append(((blk[:, None] % 2 == 1) & (blk[None, :] == blk[:, None] - 1)).T)
        mats.append(np.concatenate(rows + pins + sxs, axis=0).astype(np.float32))
        masks.append(np.stack(lm).astype(np.float32))
    return np.stack(mats), np.stack(masks)


def _rwkv_consts():
    c = CHUNK
    out = []
    for rev in (False, True):
        pos, incl, strict = _order_consts(rev)
        b16 = pos // 16
        b32 = pos // 32
        same16 = b16[None, :] == b16[:, None]
        same32 = b32[None, :] == b32[:, None]
        out.append(np.stack([incl, strict, strict & same16, strict & same32 & ~same16,
                             strict & ~same32, np.eye(c, dtype=bool)]).astype(np.float32))
    return np.stack(out)


def _gla_kernel(uf_ref, ub_ref, s0_ref, cst_ref, msk_ref, hm_ref, wd_ref, bd_ref, of_ref, ob_ref, sf_ref, s_scr,
                *, heads, dk, dv):
    ci = pl.program_id(1)

    @pl.when(ci == 0)
    def _():
        s_scr[...] = s0_ref[0]

    c = CHUNK
    qk = heads * dk
    nl = len(GLA_LEVELS)
    hm = hm_ref[...]

    def rows_per_head(x):
        return jnp.concatenate([x] * heads, axis=0)

    dirs = (0, 1)
    chains = [(d, (o if d == 0 else GLA_STEP_CHUNKS - 1 - o)) for o in range(GLA_STEP_CHUNKS) for d in dirs]
    nch = range(len(chains))
    us = [(uf_ref, ub_ref)[d][0, rb * c:(rb + 1) * c, :] for d, rb in chains]
    q = [u[:, 0:qk] * (dk ** -0.5) for u in us]
    k = [u[:, qk:2 * qk] for u in us]
    vb = [u[:, 2 * qk:2 * qk + heads * dv].astype(BF16) for u in us]
    gk = [_log_sigmoid(_dot(us[i][:, 2 * qk + 2 * heads * dv:], wd_ref[chains[i][0]], precision=HI)
                       + bd_ref[chains[i][0]]) * (1.0 / GLA_NORMALIZER) for i in nch]
    e = [jnp.exp(_split_dot_l(cst_ref[chains[i][0]], gk[i])) for i in nch]
    qm = [rows_per_head(q[i]) * hm for i in nch]
    kb = [k[i].astype(BF16) for i in nch]
    attn = [_dot(kb[i], qm[i].astype(BF16), NT) * msk_ref[chains[i][0], 0] for i in nch]
    attn = [attn[i] + _dot(kb[i], (qm[i] * rows_per_head(jnp.exp(gk[i]))).astype(BF16), NT) * msk_ref[chains[i][0], 1]
            for i in nch]
    for li in range(nl):
        for i in nch:
            qs = qm[i] * rows_per_head(e[i][(2 + li) * c:(3 + li) * c])
            ks = k[i] * e[i][(2 + nl + li) * c:(3 + nl + li) * c]
            attn[i] = attn[i] + _dot(ks.astype(BF16), qs.astype(BF16), NT) * msk_ref[chains[i][0], 2 + li]
    o_intra = [_dot(attn[i].astype(BF16), vb[i], TN) for i in nch]
    q_cum = [(qm[i] * rows_per_head(e[i][0:c])).astype(BF16) for i in nch]
    k_rest = [(rows_per_head(k[i] * e[i][c:2 * c]) * hm).astype(BF16) for i in nch]
    etot = [jnp.exp(jnp.sum(gk[i], axis=0, keepdims=True)) for i in nch]
    st = [s_scr[d] for d in dirs]
    for i in nch:
        d, rb = chains[i]
        o_inter = _dot(q_cum[i], st[d].astype(BF16), NT)
        (of_ref, ob_ref)[d][0, rb * c:(rb + 1) * c, :] = jnp.concatenate(
            [o_inter[h * c:(h + 1) * c] + o_intra[i][h * c:(h + 1) * c, h * dv:(h + 1) * dv] for h in range(heads)],
            axis=-1)
        vstack = jnp.concatenate([vb[i][:, h * dv:(h + 1) * dv] for h in range(heads)], axis=0)
        st[d] = st[d] * etot[i] + _dot(vstack, k_rest[i], TN)
    for d in dirs:
        s_scr[d] = st[d]

    @pl.when(ci == pl.num_programs(1) - 1)
    def _():
        sf_ref[0] = s_scr[...]


def gla(u_gla, s0c, w_dec, b_dec, heads, dk, dv):
    b, t, cols = u_gla.shape
    blk = GLA_STEP_CHUNKS * CHUNK
    nc = t // blk
    qk = heads * dk
    rank = w_dec.shape[1]
    cst, msk = _gla_consts()
    msk = np.tile(msk, (1, 1, 1, heads))
    hm = np.kron(np.eye(heads, dtype=np.float32), np.ones((CHUNK, dk), np.float32))
    wd = jnp.zeros((2, LANES, qk), F32)
    wd = wd.at[0, 0:rank].set(w_dec[0]).at[1, rank:2 * rank].set(w_dec[1])
    full = lambda a: pl.BlockSpec(a.shape, lambda i, j: (0,) * a.ndim)
    consts = [jnp.asarray(cst), jnp.asarray(msk), jnp.asarray(hm), wd, b_dec.reshape(2, 1, qk)]
    return pl.pallas_call(
        functools.partial(_gla_kernel, heads=heads, dk=dk, dv=dv),
        grid=(b, nc),
        in_specs=[pl.BlockSpec((1, blk, cols), lambda i, j: (i, j, 0)),
                  pl.BlockSpec((1, blk, cols), lambda i, j: (i, nc - 1 - j, 0)),
                  pl.BlockSpec((1, 2, dv, qk), lambda i, j: (i, 0, 0, 0))] + [full(a) for a in consts],
        out_specs=[pl.BlockSpec((1, blk, heads * dv), lambda i, j: (i, j, 0)),
                   pl.BlockSpec((1, blk, heads * dv), lambda i, j: (i, nc - 1 - j, 0)),
                   pl.BlockSpec((1, 2, dv, qk), lambda i, j: (i, 0, 0, 0))],
        out_shape=[jax.ShapeDtypeStruct((b, t, heads * dv), F32),
                   jax.ShapeDtypeStruct((b, t, heads * dv), F32),
                   jax.ShapeDtypeStruct((b, 2, dv, qk), F32)],
        scratch_shapes=[pltpu.VMEM((2, dv, qk), F32)],
        compiler_params=_cparams(("parallel", "arbitrary")),
    )(u_gla, u_gla, s0c, *consts)


def _gla_state_in(s):
    h, dk, dv = s.shape[-3:]
    return jnp.moveaxis(s, -1, -3).reshape(s.shape[:-3] + (dv, h * dk))


def _gla_state_out(s, heads):
    dv, qk = s.shape[-2:]
    return jnp.moveaxis(s.reshape(s.shape[:-2] + (dv, heads, qk // heads)), -3, -1)


def _dot3(x, w_ref):
    hi, lo = _two_terms(x)
    return _dot(hi, w_ref[0]) + _dot(lo, w_ref[0]) + _dot(hi, w_ref[1])


def _rwkv_kernel(uf_ref, ub_ref, s0_ref, msk_ref, w2_ref, w0_ref, a2_ref, a0_ref, g2_ref, kk_ref, ka_ref, rk_ref,
                 bones_ref, yf_ref, yb_ref, bonus_ref, gate_ref, sf_ref, s_scr, *, heads, n):
    ci = pl.program_id(1)
    g2h = 2 * heads

    @pl.when(ci == 0)
    def _():
        s_scr[...] = s0_ref[0].reshape(g2h, n, n)

    c = CHUNK
    wd = heads * n
    half = bones_ref.shape[0]
    bones = bones_ref[...]
    dirs = (0, 1)
    chains = [(d, (o if d == 0 else RWKV_STEP_CHUNKS - 1 - o)) for o in range(RWKV_STEP_CHUNKS) for d in dirs]
    nch = range(len(chains))

    def seg_sum(x):
        return jnp.concatenate([_split_dot(x[:, i:i + half], bones) for i in range(0, wd, half)], axis=-1)

    def split_heads(x):
        return jnp.stack([x[:, h * n:(h + 1) * n] for h in range(heads)], axis=0)

    def masked(x, mi):
        return jnp.concatenate([jnp.where(msk_ref[chains[i][0], mi] > 0, x[i * heads:(i + 1) * heads], 0.0)
                                for i in nch], axis=0)

    us = [(uf_ref, ub_ref)[d][0, rb * c:(rb + 1) * c, :] for d, rb in chains]
    z0 = 3 * wd
    r = [u[:, 0:wd] for u in us]
    k = [u[:, wd:2 * wd] for u in us]
    v = [u[:, 2 * wd:3 * wd] for u in us]
    wlog = [-_softplus(-(w0_ref[chains[i][0]] + _dot3(jnp.tanh(us[i][:, z0 + chains[i][0] * n:z0 + (chains[i][0] + 1) * n]),
                                                      w2_ref.at[chains[i][0]]))) - 0.5 for i in nch]
    logw = [-jnp.exp(wlog[i]) for i in nch]
    a = [_sigmoid(a0_ref[...] + _dot3(us[i][:, z0 + 2 * n:z0 + 3 * n], a2_ref)) for i in nch]
    kk = [k[i] * kk_ref[...] for i in nch]
    kk = [kk[i] * lax.rsqrt(jnp.maximum(seg_sum(kk[i] * kk[i]), 1e-24)) for i in nch]
    km = [k[i] * (1.0 + (a[i] - 1.0) * ka_ref[...]) for i in nch]
    bb = [kk[i] * a[i] for i in nch]
    cum = [_split_dot_l(msk_ref[chains[i][0], 0], logw[i]) for i in nch]
    tot = [jnp.sum(logw[i], axis=0, keepdims=True) for i in nch]
    e_neg = [jnp.exp(-cum[i]) for i in nch]
    e_rest = [jnp.exp(tot[i] - cum[i]) for i in nch]
    e_tot = [jnp.exp(tot[i]) for i in nch]
    ar = jnp.concatenate([split_heads(jnp.concatenate([kk[i] * jnp.exp(cum[i] - logw[i]), r[i] * jnp.exp(cum[i])],
                                                      axis=0)) for i in nch], axis=0)
    kb = jnp.concatenate([split_heads(jnp.concatenate([km[i] * e_neg[i], bb[i] * e_neg[i]], axis=0)) for i in nch],
                         axis=0)
    vv = jnp.concatenate([split_heads(v[i]) for i in nch], axis=0)
    kbh = [jnp.concatenate([km[i] * e_rest[i], bb[i] * e_rest[i]], axis=0) for i in nch]
    for i in nch:
        d, rb = chains[i]
        if d == 0:
            bonus_ref[0, rb * c:(rb + 1) * c, :] = seg_sum(r[i] * km[i] * rk_ref[...]) * v[i]
            gate_ref[0, rb * c:(rb + 1) * c, :] = _dot3(
                _sigmoid(us[i][:, z0 + 3 * n:z0 + 3 * n + g2_ref.shape[1]]), g2_ref)

    sc = _bmm(ar, kb, BNT)
    lak = masked(sc[:, :c, :c], 1)
    lab = masked(sc[:, :c, c:], 1)
    mrk = masked(sc[:, c:, :c], 0)
    mrb = masked(sc[:, c:, c:], 0)
    lv = _bmm(lak, vv)
    ld = masked(lab, 2)
    x = msk_ref[0, 5] - ld
    p = _bmm(ld, ld)
    x = x + _bmm(x, p)
    p = _bmm(p, p)
    x = x + _bmm(x, p)
    p = _bmm(p, p)
    x = x + _bmm(x, p)
    for mi in (3, 4):
        x = x - _bmm(x, _bmm(masked(lab, mi), x))

    st = s_scr[...]
    for o in range(RWKV_STEP_CHUNKS):
        sl = slice(o * g2h, (o + 1) * g2h)
        through = _bmm(ar[sl], st, BNT)
        uu = _bmm(x[sl], through[:, :c] + lv[sl])
        vu = jnp.concatenate([vv[sl], -uu], axis=1)
        y = through[:, c:] + _bmm(jnp.concatenate([mrk[sl], mrb[sl]], axis=2), vu)
        new_st = []
        for d in dirs:
            i = o * 2 + d
            rb = chains[i][1]
            (yf_ref, yb_ref)[d][0, rb * c:(rb + 1) * c, :] = jnp.concatenate(
                [y[d * heads + h] for h in range(heads)], axis=-1)
            for h in range(heads):
                hs = slice(h * n, (h + 1) * n)
                g = d * heads + h
                new_st.append(st[g] * e_tot[i][:, hs] + _bdot(vu[g], kbh[i][:, hs], TN))
        st = jnp.stack(new_st, axis=0)
    s_scr[...] = st

    @pl.when(ci == pl.num_programs(1) - 1)
    def _():
        sf_ref[0] = s_scr[...].reshape(2, heads, n, n)


def rwkv(u, s0, w2, w0, a2, a0, g2, k_k, k_a, r_k, heads, n):
    b, t, cols = u.shape
    blk = RWKV_STEP_CHUNKS * CHUNK
    nc = t // blk
    wd = heads * n
    msk = _rwkv_consts()
    bones = np.kron(np.eye(heads // 2, dtype=np.float32), np.ones((n, n), np.float32))
    row = lambda x: x.reshape(1, wd)
    full = lambda a: pl.BlockSpec(a.shape, lambda i, j: (0,) * a.ndim)
    terms = lambda w: jnp.stack(_two_terms(w), axis=-3)
    args = [jnp.asarray(msk), terms(w2), w0.reshape(2, 1, wd), terms(a2), row(a0), terms(g2), row(k_k), row(k_a),
            row(r_k), jnp.asarray(bones)]
    return pl.pallas_call(
        functools.partial(_rwkv_kernel, heads=heads, n=n),
        grid=(b, nc),
        in_specs=[pl.BlockSpec((1, blk, cols), lambda i, j: (i, j, 0)),
                  pl.BlockSpec((1, blk, cols), lambda i, j: (i, nc - 1 - j, 0)),
                  pl.BlockSpec((1, 2, heads, n, n), lambda i, j: (i, 0, 0, 0, 0))] + [full(a) for a in args],
        out_specs=[pl.BlockSpec((1, blk, wd), lambda i, j: (i, j, 0)),
                   pl.BlockSpec((1, blk, wd), lambda i, j: (i, nc - 1 - j, 0)),
                   pl.BlockSpec((1, blk, wd), lambda i, j: (i, j, 0)),
                   pl.BlockSpec((1, blk, wd), lambda i, j: (i, j, 0)),
                   pl.BlockSpec((1, 2, heads, n, n), lambda i, j: (i, 0, 0, 0, 0))],
        out_shape=[jax.ShapeDtypeStruct((b, t, wd), F32)] * 4
        + [jax.ShapeDtypeStruct((b, 2, heads, n, n), F32)],
        scratch_shapes=[pltpu.VMEM((2 * heads, n, n), F32)],
        compiler_params=_cparams(("parallel", "arbitrary")),
    )(u, u, s0, *args)


def _proj_out_kernel(x_ref, mod_ref, of_ref, ob_ref, gg_ref, gn_ref, yf_ref, yb_ref, bonus_ref, gate_ref,
                     lng_ref, lnb_ref, bmean_ref, w_ref, o_ref, *, gla_heads):
    m = mod_ref[0]
    o = of_ref[0] + ob_ref[0]
    gw = o.shape[-1]
    dv = gw // gla_heads
    parts = []
    for h in range(gla_heads):
        oh = o[:, h * dv:(h + 1) * dv]
        parts.append(oh * lax.rsqrt(jnp.mean(oh * oh, axis=-1, keepdims=True) + EPS))
    o_gla = jnp.concatenate(parts, axis=-1) * gn_ref[...] * _silu(gg_ref[0])
    y = yf_ref[0] + yb_ref[0]
    bmean = bmean_ref[...]
    dy = y - _split_dot(y, bmean)
    var = _split_dot(dy * dy, bmean)
    yn = dy * lax.rsqrt(var + RWKV_LN_EPS) * lng_ref[...] + lnb_ref[...]
    o_rwkv = (yn + bonus_ref[0]) * gate_ref[0]
    w = w_ref[...]
    mix = _bdot(o_gla, w[:gw]) + _bdot(o_rwkv, w[gw:])
    o_ref[0] = x_ref[0] + m[2:3] * mix


def proj_out(x, mod, mod_row, o_f, o_b, u_gla, gla_norm, yf, yb, bonus, gate, ln_g, ln_b, w_out, gla_heads, rwkv_n,
             tm=256):
    b, t, d = x.shape
    gw = o_f.shape[-1]
    rw = yf.shape[-1]
    tm = min(tm, t)
    bmean = np.kron(np.eye(rw // rwkv_n, dtype=np.float32), np.full((rwkv_n, rwkv_n), 1.0 / rwkv_n, np.float32))
    tile = lambda wdt: pl.BlockSpec((1, tm, wdt), lambda i, j: (i, j, 0))
    row = lambda wdt: pl.BlockSpec((1, wdt), lambda i, j: (0, 0))
    g_col_block = (u_gla.shape[-1] - LANES - gw) // gw
    return pl.pallas_call(
        functools.partial(_proj_out_kernel, gla_heads=gla_heads),
        grid=(b, t // tm),
        in_specs=[tile(d),
                  pl.BlockSpec((1, 6, d), lambda i, j: (mod_row(i), 0, 0)),
                  tile(gw), tile(gw),
                  pl.BlockSpec((1, tm, gw), lambda i, j: (i, j, g_col_block)),
                  row(gw), tile(rw), tile(rw), tile(rw), tile(rw), row(rw), row(rw),
                  pl.BlockSpec((rw, rw), lambda i, j: (0, 0)),
                  pl.BlockSpec(w_out.shape, lambda i, j: (0, 0))],
        out_specs=tile(d),
        out_shape=jax.ShapeDtypeStruct((b, t, d), F32),
        compiler_params=_cparams(("parallel", "parallel")),
    )(x, mod, o_f, o_b, u_gla, gla_norm.reshape(1, gw), yf, yb, bonus, gate, ln_g.reshape(1, rw),
      ln_b.reshape(1, rw), jnp.asarray(bmean), w_out)


ROUTE_TILE = 256
MOE_SUB = 128
MOE_CAP = 32
MOE_SUPER = 1024


def _route_kernel(x_ref, mod_ref, g_ref, rw_ref, rb_ref, tri_ref, h_ref, gates_ref, pos_ref, *, n_exp):
    m = mod_ref[0]
    h = _rms_mod(x_ref[...], g_ref[...], m[4:5], m[3:4])
    h_ref[...] = h.astype(BF16)
    tm = h.shape[0]
    gsz = n_exp // N_GROUPS
    scores = _sigmoid(_dot(rw_ref[...], h, NT, precision=HI))
    sel = scores + rb_ref[...]
    neg = jnp.float32(-jnp.inf)
    s3 = sel.reshape(N_GROUPS, gsz, tm)
    idx = lax.broadcasted_iota(jnp.int32, s3.shape, 1)
    m1 = jnp.max(s3, axis=1, keepdims=True)
    first = jnp.min(jnp.where(s3 == m1, idx, gsz), axis=1, keepdims=True)
    m2 = jnp.max(jnp.where(idx == first, neg, s3), axis=1, keepdims=True)
    gs = (m1 + m2).reshape(N_GROUPS, tm)
    gidx = lax.broadcasted_iota(jnp.int32, gs.shape, 0)
    cnt = jnp.zeros(gs.shape, jnp.int32)
    for g in range(N_GROUPS):
        other = gs[g:g + 1]
        cnt = cnt + ((other > gs) | ((other == gs) & (g < gidx))).astype(jnp.int32)
    gkeep = jnp.broadcast_to((cnt < TOPK_GROUPS)[:, None, :], s3.shape).reshape(n_exp, tm)
    selm = jnp.where(gkeep, sel, neg)
    eidx = lax.broadcasted_iota(jnp.int32, selm.shape, 0)
    rank = jnp.zeros(selm.shape, jnp.int32)
    for e in range(n_exp):
        other = selm[e:e + 1]
        rank = rank + ((other > selm) | ((other == selm) & (e < eidx))).astype(jnp.int32)
    chosen = rank < TOP_K
    wts = jnp.where(chosen, scores, 0.0)
    gates_ref[...] = wts / jnp.sum(wts, axis=0, keepdims=True) * ROUTED_SCALE
    pos_ref[...] = _dot(chosen.astype(BF16), tri_ref[...])


def route(x2, mod, mod_row, norm_g, router_w, router_b, tm, sub):
    nt, d = x2.shape
    n_exp = router_w.shape[1]
    tri = np.triu(np.ones((tm, tm), np.float32), 1) * np.kron(np.eye(tm // sub), np.ones((sub, sub)))
    return pl.pallas_call(
        functools.partial(_route_kernel, n_exp=n_exp),
        grid=(nt // tm,),
        in_specs=[pl.BlockSpec((tm, d), lambda i: (i, 0)),
                  pl.BlockSpec((1, 6, d), lambda i: (mod_row(i, tm), 0, 0)),
                  pl.BlockSpec((1, d), lambda i: (0, 0)),
                  pl.BlockSpec((n_exp, d), lambda i: (0, 0)),
                  pl.BlockSpec((n_exp, 1), lambda i: (0, 0)),
                  pl.BlockSpec((tm, tm), lambda i: (0, 0))],
        out_specs=[pl.BlockSpec((tm, d), lambda i: (i, 0)),
                   pl.BlockSpec((n_exp, tm), lambda i: (0, i)),
                   pl.BlockSpec((n_exp, tm), lambda i: (0, i))],
        out_shape=[jax.ShapeDtypeStruct((nt, d), BF16),
                   jax.ShapeDtypeStruct((n_exp, nt), F32),
                   jax.ShapeDtypeStruct((n_exp, nt), F32)],
        compiler_params=_cparams(("parallel",)),
    )(x2, mod, norm_g.reshape(1, d), router_w.T, router_b.reshape(n_exp, 1), jnp.asarray(tri, BF16))


def _swiglu(h, w1, w3, w2):
    return _bdot(_silu(_bdot(h, w1)) * _bdot(h, w3), w2)


def _moe_kernel(npass_ref, x_ref, mod_ref, h_ref, gates_ref, pos_ref, w1_ref, w3_ref, w2_ref, s1_ref, s3_ref, s2_ref,
                o_ref, xe_ref, ye_ref, *, sub, cap, group):
    i = pl.program_id(0)
    j = pl.program_id(1)
    n_grp = pl.num_programs(1)
    n_sub = h_ref.shape[0] // sub

    @pl.when(j == 0)
    def _():
        o_ref[...] = _swiglu(h_ref[...], s1_ref[...], s3_ref[...], s2_ref[...])

    first = (i * n_grp + j) * group
    gate = [gates_ref[pl.ds(j * group + x, 1), :] for x in range(group)]
    pos = [pos_ref[pl.ds(j * group + x, 1), :] for x in range(group)]
    need = [npass_ref[first + x] for x in range(group)]
    rows = lax.broadcasted_iota(jnp.int32, (cap, sub), 0).astype(F32)

    def one_pass(it, skip_done):
        base = it * float(cap) if isinstance(it, int) else (it * cap).astype(F32)
        picks = []
        for x in range(group):
            picks.append([])
            for s in range(n_sub):
                sl = slice(s * sub, (s + 1) * sub)
                g = gate[x][:, sl]
                pick = jnp.where((pos[x][:, sl] - base == rows) & (g > 0.0), g, 0.0)
                picks[x].append(pick.astype(BF16))

            def run_expert(x=x):
                for s in range(n_sub):
                    onehot = (picks[x][s] > 0.0).astype(BF16)
                    xe_ref[x, s * cap:(s + 1) * cap, :] = _dot(onehot, h_ref[s * sub:(s + 1) * sub, :]).astype(BF16)
                ye_ref[x] = _swiglu(xe_ref[x], w1_ref[x], w3_ref[x], w2_ref[x]).astype(BF16)

            if skip_done:
                pl.when(it < need[x])(run_expert)
        if not skip_done:
            for s in range(n_sub):
                onehot = jnp.concatenate([(picks[x][s] > 0.0).astype(BF16) for x in range(group)], axis=0)
                rows_all = _dot(onehot, h_ref[s * sub:(s + 1) * sub, :]).astype(BF16)
                for x in range(group):
                    xe_ref[x, s * cap:(s + 1) * cap, :] = rows_all[x * cap:(x + 1) * cap]
            mid = [_silu(_bdot(xe_ref[x], w1_ref[x])) * _bdot(xe_ref[x], w3_ref[x]) for x in range(group)]
            for x in range(group):
                ye_ref[x] = _bdot(mid[x], w2_ref[x]).astype(BF16)
        for s in range(n_sub):
            sl = slice(s * sub, (s + 1) * sub)
            pick_all = jnp.concatenate([picks[x][s] for x in range(group)], axis=0)
            ye_all = jnp.concatenate([ye_ref[x, s * cap:(s + 1) * cap, :] for x in range(group)], axis=0)
            o_ref[sl, :] += _dot(pick_all, ye_all, TN)

    one_pass(0, False)
    n_pass = need[0]
    for x in range(1, group):
        n_pass = jnp.maximum(n_pass, need[x])

    def later_pass(it, carry):
        one_pass(it, True)
        return carry

    lax.fori_loop(1, n_pass, later_pass, 0)

    @pl.when(j == n_grp - 1)
    def _():
        o_ref[...] = x_ref[...] + mod_ref[0][5:6] * o_ref[...]


def moe(x2, mod, mod_row, h, gates, pos, w1, w3, w2, s1, s3, s2, sub, tm):
    nt, d = x2.shape
    n_exp, _, ff = w1.shape
    cap = min(MOE_CAP, sub)
    group = LANES // cap
    n_sub = tm // sub
    cnt = (pos + (gates > 0.0))[:, sub - 1::sub].reshape(n_exp, nt // tm, n_sub).max(axis=-1)
    npass = ((cnt.astype(jnp.int32) + cap - 1) // cap).T.reshape(-1)
    full = lambda a: pl.BlockSpec(a.shape, lambda i, j, n: (0,) * a.ndim)
    grid_spec = pltpu.PrefetchScalarGridSpec(
        num_scalar_prefetch=1,
        grid=(nt // tm, n_exp // group),
        in_specs=[pl.BlockSpec((tm, d), lambda i, j, n: (i, 0)),
                  pl.BlockSpec((1, 6, d), lambda i, j, n: (mod_row(i, tm), 0, 0)),
                  pl.BlockSpec((tm, d), lambda i, j, n: (i, 0)),
                  pl.BlockSpec((n_exp, tm), lambda i, j, n: (0, i)),
                  pl.BlockSpec((n_exp, tm), lambda i, j, n: (0, i)),
                  pl.BlockSpec((group, d, ff), lambda i, j, n: (j, 0, 0)),
                  pl.BlockSpec((group, d, ff), lambda i, j, n: (j, 0, 0)),
                  pl.BlockSpec((group, ff, d), lambda i, j, n: (j, 0, 0)),
                  full(s1), full(s3), full(s2)],
        out_specs=pl.BlockSpec((tm, d), lambda i, j, n: (i, 0)),
        scratch_shapes=[pltpu.VMEM((group, n_sub * cap, d), BF16), pltpu.VMEM((group, n_sub * cap, d), BF16)])
    return pl.pallas_call(
        functools.partial(_moe_kernel, sub=sub, cap=cap, group=group),
        grid_spec=grid_spec,
        out_shape=jax.ShapeDtypeStruct((nt, d), F32),
        compiler_params=_cparams(("parallel", "arbitrary")),
    )(npass, x2, mod, h, gates, pos, w1, w3, w2, s1, s3, s2)


def _final_kernel(x_ref, g_ref, o_ref):
    x = x_ref[...]
    o_ref[...] = x * lax.rsqrt(jnp.mean(x * x, axis=-1, keepdims=True) + EPS) * g_ref[...]


def final_norm(x2, g, tm=512):
    nt, d = x2.shape
    tm = min(tm, nt)
    return pl.pallas_call(
        _final_kernel,
        grid=(nt // tm,),
        in_specs=[pl.BlockSpec((tm, d), lambda i: (i, 0)), pl.BlockSpec((1, d), lambda i: (0, 0))],
        out_specs=pl.BlockSpec((tm, d), lambda i: (i, 0)),
        out_shape=jax.ShapeDtypeStruct((nt, d), F32),
        compiler_params=_cparams(("parallel",)),
    )(x2, g.reshape(1, d))


def _pad_cols(a, n):
    return jnp.pad(a, [(0, 0)] * (a.ndim - 1) + [(0, n - a.shape[-1])])


def _round_up(n, m):
    return -(-n // m) * m


def _layer(x, mod, latent, s_gla_c, s_rwkv, p, dims, grid_w):
    b, t, d = x.shape
    ctx_row = mod.shape[0] - 1
    brow = (lambda i: i) if latent else (lambda i: ctx_row)
    trow = (lambda i, tm: (i * tm) // t) if latent else (lambda i, tm: ctx_row)
    u_gla, u_rwkv = proj_in(x, mod, brow, p['norm1'], p['w_in'], dims['gla_cols'])
    hh, ww = (t // grid_w, grid_w) if latent else (b, t)
    u_rwkv = conv(u_rwkv, p['rwkv_conv'], hh, ww, vertical=latent)
    o_f, o_b, sg = gla(u_gla, s_gla_c, p['gla_w_dec'], p['gla_b_dec'], dims['gla_heads'], dims['gla_dk'], dims['gla_dv'])
    yf, yb, bonus, gate, sr = rwkv(u_rwkv, s_rwkv, p['rwkv_w2'], p['rwkv_w0'], p['rwkv_a2'], p['rwkv_a0'],
                                   p['rwkv_g2'], p['rwkv_k_k'], p['rwkv_k_a'], p['rwkv_r_k'],
                                   dims['rwkv_heads'], dims['rwkv_n'])
    x = proj_out(x, mod, brow, o_f, o_b, u_gla, p['gla_norm'], yf, yb, bonus, gate, p['rwkv_ln_g'], p['rwkv_ln_b'],
                 p['w_out'], dims['gla_heads'], dims['rwkv_n'])
    x2 = x.reshape(b * t, d)
    sub = min(MOE_SUB, t if latent else b * t)
    tm = min(MOE_SUPER, t if latent else b * t)
    h, gates, pos = route(x2, mod, trow, p['norm2'], p['router_w'], p['router_b'], min(ROUTE_TILE, tm), sub)
    x2 = moe(x2, mod, trow, h, gates, pos, p['exp_w1'], p['exp_w3'], p['exp_w2'], p['sh_w1'], p['sh_w3'], p['sh_w2'],
             sub, tm)
    return x2.reshape(b, t, d), sg, sr


def kernel(x_prompt, x_sample, c, state_gla, state_rwkv, c_ctx, ada_w, ada_b, norm1, norm2, norm_f, w_in, w_out,
           gla_w_dec, gla_b_dec, gla_norm, rwkv_conv, rwkv_w2, rwkv_w0, rwkv_a2, rwkv_a0, rwkv_g2, rwkv_k_k, rwkv_k_a,
           rwkv_r_k, rwkv_ln_g, rwkv_ln_b, router_w, router_b, exp_w1, exp_w3, exp_w2, sh_w1, sh_w3, sh_w2):
    depth, d, _ = ada_w.shape
    bp = x_prompt.shape[0]
    bs = x_sample.shape[0]
    grid_w = 64
    gla_heads, gla_dk, gla_dv = state_gla.shape[3:]
    rwkv_heads, rwkv_n = state_rwkv.shape[3:5]
    gla_qk, gla_w, rwkv_w = gla_heads * gla_dk, gla_heads * gla_dv, rwkv_heads * rwkv_n
    gla_cols_raw = 2 * gla_qk + 2 * gla_w + 2 * gla_w_dec.shape[2]
    rwkv_cols_raw = w_in.shape[2] - gla_cols_raw
    gla_cols = 2 * gla_qk + 2 * gla_w + LANES
    rwkv_cols = _round_up(rwkv_cols_raw, LANES)
    dims = dict(gla_cols=gla_cols, gla_heads=gla_heads, gla_dk=gla_dk, gla_dv=gla_dv,
                rwkv_heads=rwkv_heads, rwkv_n=rwkv_n)

    w_in_p = jnp.concatenate([_pad_cols(w_in[..., :gla_cols_raw], gla_cols),
                              _pad_cols(w_in[..., gla_cols_raw:], rwkv_cols)], axis=-1).astype(BF16)
    conv_p = _pad_cols(rwkv_conv, rwkv_cols)
    w_out_b = w_out.astype(BF16)
    e1, e3, e2 = exp_w1.astype(BF16), exp_w3.astype(BF16), exp_w2.astype(BF16)
    s1, s3, s2 = sh_w1.astype(BF16), sh_w3.astype(BF16), sh_w2.astype(BF16)

    rows = _round_up(bs + 1, 8)
    cvec = jnp.concatenate([c, c_ctx[None], jnp.zeros((rows - bs - 1, d), F32)], axis=0)
    mod_all = adaln(cvec, ada_w, ada_b).reshape(depth, rows, 6, d)[:, :bs + 1]

    zero_gla = jnp.zeros((bp, 2, gla_dv, gla_qk), F32)
    zero_rwkv = jnp.zeros((bp, 2, rwkv_heads, rwkv_n, rwkv_n), F32)
    state_gla_c = _gla_state_in(state_gla)

    xp, xs = x_prompt, x_sample
    new_gla, new_rwkv = [], []
    for l in range(depth):
        p = dict(norm1=norm1[l], norm2=norm2[l], w_in=w_in_p[l], w_out=w_out_b[l], gla_w_dec=gla_w_dec[l],
                 gla_b_dec=gla_b_dec[l], gla_norm=gla_norm[l], rwkv_conv=conv_p[l], rwkv_w2=rwkv_w2[l],
                 rwkv_w0=rwkv_w0[l], rwkv_a2=rwkv_a2[l], rwkv_a0=rwkv_a0[l], rwkv_g2=rwkv_g2[l],
                 rwkv_k_k=rwkv_k_k[l], rwkv_k_a=rwkv_k_a[l], rwkv_r_k=rwkv_r_k[l], rwkv_ln_g=rwkv_ln_g[l],
                 rwkv_ln_b=rwkv_ln_b[l], router_w=router_w[l], router_b=router_b[l], exp_w1=e1[l], exp_w3=e3[l],
                 exp_w2=e2[l], sh_w1=s1[l], sh_w3=s3[l], sh_w2=s2[l])
        mod = mod_all[l]
        xp, sg, sr = _layer(xp, mod, False, zero_gla, zero_rwkv, p, dims, grid_w)
        new_gla.append(_gla_state_out(sg, gla_heads))
        new_rwkv.append(sr)
        xs, _, _ = _layer(xs, mod, True, state_gla_c[:, l], state_rwkv[:, l], p, dims, grid_w)
    y_prompt = final_norm(xp.reshape(-1, d), norm_f).reshape(xp.shape)
    y_sample = final_norm(xs.reshape(-1, d), norm_f).reshape(xs.shape)
    return (y_prompt, y_sample, jnp.stack(new_gla, axis=1), jnp.stack(new_rwkv, axis=1))
```

```python
import functools

import numpy as np
import jax
import jax.numpy as jnp
from jax import lax
from jax.experimental import pallas as pl
from jax.experimental.pallas import tpu as pltpu

F32 = jnp.float32
BF16 = jnp.bfloat16
HI = lax.Precision.HIGHEST

LANES = 128
EPS = 1e-6
RWKV_LN_EPS = 64e-5
GLA_NORMALIZER = 16.0
ROUTED_SCALE = 2.5
N_GROUPS = 8
TOPK_GROUPS = 4
TOP_K = 8
CHUNK = 64
RWKV_STEP_CHUNKS = 4
VMEM_LIMIT = 56 * 1024 * 1024


def _cparams(sem):
    return pltpu.CompilerParams(dimension_semantics=sem, vmem_limit_bytes=VMEM_LIMIT)


def _dot(a, b, dims=(((1,), (0,)), ((), ())), precision=None):
    return lax.dot_general(a, b, dims, precision=precision, preferred_element_type=F32)


def _bdot(a, b, dims=(((1,), (0,)), ((), ()))):
    return lax.dot_general(a.astype(BF16), b.astype(BF16), dims, preferred_element_type=F32)


NT = (((1,), (1,)), ((), ()))
TN = (((0,), (0,)), ((), ()))
BNN = (((2,), (1,)), ((0,), (0,)))
BNT = (((2,), (2,)), ((0,), (0,)))


def _bmm(a, b, dims=BNN):
    return lax.dot_general(a.astype(BF16), b.astype(BF16), dims, preferred_element_type=F32)


def _two_terms(x):
    hi = x.astype(BF16)
    return hi, (x - hi.astype(F32)).astype(BF16)


def _split_dot(x, m):
    hi, lo = _two_terms(x)
    mb = m.astype(BF16)
    return _dot(hi, mb) + _dot(lo, mb)


def _split_dot_l(m, x):
    hi, lo = _two_terms(x)
    mb = m.astype(BF16)
    return _dot(mb, hi) + _dot(mb, lo)


def _sigmoid(x):
    return 1.0 / (1.0 + jnp.exp(-x))


def _silu(x):
    return x * _sigmoid(x)


def _softplus(x):
    return jnp.maximum(x, 0.0) + jnp.log(1.0 + jnp.exp(-jnp.abs(x)))


def _log_sigmoid(x):
    return -_softplus(-x)


def _rms_mod(x, g, scale, shift):
    y = x * lax.rsqrt(jnp.mean(x * x, axis=-1, keepdims=True) + EPS)
    return (y * g) * (1.0 + scale) + shift


def _adaln_kernel(c_ref, w_ref, b_ref, o_ref):
    c = c_ref[...]
    o_ref[0] = _dot(_silu(c), w_ref[0], precision=HI) + b_ref[0]


def adaln(cvec, ada_w, ada_b, tn=512):
    depth, d, n = ada_w.shape
    r = cvec.shape[0]
    return pl.pallas_call(
        _adaln_kernel,
        grid=(depth, n // tn),
        in_specs=[pl.BlockSpec((r, d), lambda l, j: (0, 0)),
                  pl.BlockSpec((1, d, tn), lambda l, j: (l, 0, j)),
                  pl.BlockSpec((1, 1, tn), lambda l, j: (l, 0, j))],
        out_specs=pl.BlockSpec((1, r, tn), lambda l, j: (l, 0, j)),
        out_shape=jax.ShapeDtypeStruct((depth, r, n), F32),
        compiler_params=_cparams(("parallel", "parallel")),
    )(cvec, ada_w, ada_b.reshape(depth, 1, n))


def _proj_in_kernel(x_ref, mod_ref, g_ref, w_ref, og_ref, or_ref, *, n_gla):
    m = mod_ref[0]
    h = _rms_mod(x_ref[0], g_ref[...], m[1:2], m[0:1])
    u = _bdot(h, w_ref[...])
    og_ref[0] = u[:, :n_gla]
    or_ref[0] = u[:, n_gla:]


def proj_in(x, mod, mod_row, norm_g, w, n_gla, tm=512):
    b, t, d = x.shape
    n = w.shape[1]
    tm = min(tm, t)
    return pl.pallas_call(
        functools.partial(_proj_in_kernel, n_gla=n_gla),
        grid=(b, t // tm),
        in_specs=[pl.BlockSpec((1, tm, d), lambda i, j: (i, j, 0)),
                  pl.BlockSpec((1, 6, d), lambda i, j: (mod_row(i), 0, 0)),
                  pl.BlockSpec((1, d), lambda i, j: (0, 0)),
                  pl.BlockSpec((d, n), lambda i, j: (0, 0))],
        out_specs=[pl.BlockSpec((1, tm, n_gla), lambda i, j: (i, j, 0)),
                   pl.BlockSpec((1, tm, n - n_gla), lambda i, j: (i, j, 0))],
        out_shape=[jax.ShapeDtypeStruct((b, t, n_gla), F32),
                   jax.ShapeDtypeStruct((b, t, n - n_gla), F32)],
        compiler_params=_cparams(("parallel", "parallel")),
    )(x, mod, norm_g.reshape(1, d), w)


CONV_PAD = 8


def _conv_kernel(u_ref, w_ref, o_ref, pad_ref, *, hh, ww, vertical):
    pad_ref[...] = jnp.zeros(pad_ref.shape, F32)
    pad_ref[1:hh + 1, CONV_PAD:CONV_PAD + ww, :] = u_ref[0]
    w = w_ref[...]
    acc = jnp.zeros((hh, ww, u_ref.shape[-1]), F32)
    for dy in range(3):
        if not vertical and dy != 1:
            continue
        for dx in range(3):
            acc = acc + pad_ref[dy:dy + hh, CONV_PAD - 1 + dx:CONV_PAD - 1 + dx + ww, :] * w[dy, dx]
    o_ref[0] = acc


def conv(u, wconv, hh, ww, vertical):
    c = u.shape[-1]
    b = u.size // (hh * ww * c)
    out = pl.pallas_call(
        functools.partial(_conv_kernel, hh=hh, ww=ww, vertical=vertical),
        grid=(b, c // LANES),
        in_specs=[pl.BlockSpec((1, hh, ww, LANES), lambda i, j: (i, 0, 0, j)),
                  pl.BlockSpec((3, 3, LANES), lambda i, j: (0, 0, j))],
        out_specs=pl.BlockSpec((1, hh, ww, LANES), lambda i, j: (i, 0, 0, j)),
        out_shape=jax.ShapeDtypeStruct((b, hh, ww, c), F32),
        scratch_shapes=[pltpu.VMEM((hh + 2, ww + 2 * CONV_PAD, LANES), F32)],
        compiler_params=_cparams(("parallel", "parallel")),
    )(u.reshape(b, hh, ww, c), wconv)
    return out.reshape(u.shape)


GLA_LEVELS = (32, 16, 8, 4, 2)
GLA_STEP_CHUNKS = 4


def _order_consts(rev):
    c = CHUNK
    idx = np.arange(c)
    pos = (c - 1 - idx) if rev else idx
    incl = (pos[None, :] <= pos[:, None])
    strict = (pos[None, :] < pos[:, None])
    return pos, incl, strict


def _gla_consts():
    c = CHUNK
    mats, masks = [], []
    for rev in (False, True):
        pos, incl, strict = _order_consts(rev)
        later = (pos[None, :] > pos[:, None])
        rows = [incl, later]
        pins, sxs, lm = [], [], [np.eye(c, dtype=bool)]
        for s in (1,) + GLA_LEVELS:
            blk = pos // s
            same = blk[None, :] == blk[:, None]
            if s > 1:
                pins.append(same & incl)
                sxs.append(same & later)
            lm.append(((blk[:, None] % 2 == 1) & (blk[None, :] == blk[:, None] - 1)).T)
        mats.append(np.concatenate(rows + pins + sxs, axis=0).astype(np.float32))
        masks.append(np.stack(lm).astype(np.float32))
    return np.stack(mats), np.stack(masks)


def _rwkv_consts():
    c = CHUNK
    out = []
    for rev in (False, True):
        pos, incl, strict = _order_consts(rev)
        b16 = pos // 16
        b32 = pos // 32
        same16 = b16[None, :] == b16[:, None]
        same32 = b32[None, :] == b32[:, None]
        out.append(np.stack([incl, strict, strict & same16, strict & same32 & ~same16,
                             strict & ~same32, np.eye(c, dtype=bool)]).astype(np.float32))
    return np.stack(out)


def _gla_kernel(uf_ref, ub_ref, s0_ref, cst_ref, msk_ref, hm_ref, wd_ref, bd_ref, of_ref, ob_ref, sf_ref, s_scr,
                *, heads, dk, dv):
    ci = pl.program_id(1)

    @pl.when(ci == 0)
    def _():
        s_scr[...] = s0_ref[0]

    c = CHUNK
    qk = heads * dk
    nl = len(GLA_LEVELS)
    hm = hm_ref[...]

    def rows_per_head(x):
        return jnp.concatenate([x] * heads, axis=0)

    dirs = (0, 1)
    chains = [(d, (o if d == 0 else GLA_STEP_CHUNKS - 1 - o)) for o in range(GLA_STEP_CHUNKS) for d in dirs]
    nch = range(len(chains))
    us = [(uf_ref, ub_ref)[d][0, rb * c:(rb + 1) * c, :] for d, rb in chains]
    q = [u[:, 0:qk] * (dk ** -0.5) for u in us]
    k = [u[:, qk:2 * qk] for u in us]
    vb = [u[:, 2 * qk:2 * qk + heads * dv].astype(BF16) for u in us]
    gk = [_log_sigmoid(_dot(us[i][:, 2 * qk + 2 * heads * dv:], wd_ref[chains[i][0]], precision=HI)
                       + bd_ref[chains[i][0]]) * (1.0 / GLA_NORMALIZER) for i in nch]
    e = [jnp.exp(_split_dot_l(cst_ref[chains[i][0]], gk[i])) for i in nch]
    qm = [rows_per_head(q[i]) * hm for i in nch]
    kb = [k[i].astype(BF16) for i in nch]
    attn = [_dot(kb[i], qm[i].astype(BF16), NT) * msk_ref[chains[i][0], 0] for i in nch]
    attn = [attn[i] + _dot(kb[i], (qm[i] * rows_per_head(jnp.exp(gk[i]))).astype(BF16), NT) * msk_ref[chains[i][0], 1]
            for i in nch]
    for li in range(nl):
        for i in nch:
            qs = qm[i] * rows_per_head(e[i][(2 + li) * c:(3 + li) * c])
            ks = k[i] * e[i][(2 + nl + li) * c:(3 + nl + li) * c]
            attn[i] = attn[i] + _dot(ks.astype(BF16), qs.astype(BF16), NT) * msk_ref[chains[i][0], 2 + li]
    o_intra = [_dot(attn[i].astype(BF16), vb[i], TN) for i in nch]
    q_cum = [(qm[i] * rows_per_head(e[i][0:c])).astype(BF16) for i in nch]
    k_rest = [(rows_per_head(k[i] * e[i][c:2 * c]) * hm).astype(BF16) for i in nch]
    etot = [jnp.exp(jnp.sum(gk[i], axis=0, keepdims=True)) for i in nch]
    st = [s_scr[d] for d in dirs]
    for i in nch:
        d, rb = chains[i]
        o_inter = _dot(q_cum[i], st[d].astype(BF16), NT)
        (of_ref, ob_ref)[d][0, rb * c:(rb + 1) * c, :] = jnp.concatenate(
            [o_inter[h * c:(h + 1) * c] + o_intra[i][h * c:(h + 1) * c, h * dv:(h + 1) * dv] for h in range(heads)],
            axis=-1)
        vstack = jnp.concatenate([vb[i][:, h * dv:(h + 1) * dv] for h in range(heads)], axis=0)
        st[d] = st[d] * etot[i] + _dot(vstack, k_rest[i], TN)
    for d in dirs:
        s_scr[d] = st[d]

    @pl.when(ci == pl.num_programs(1) - 1)
    def _():
        sf_ref[0] = s_scr[...]


def gla(u_gla, s0c, w_dec, b_dec, heads, dk, dv):
    b, t, cols = u_gla.shape
    blk = GLA_STEP_CHUNKS * CHUNK
    nc = t // blk
    qk = heads * dk
    rank = w_dec.shape[1]
    cst, msk = _gla_consts()
    msk = np.tile(msk, (1, 1, 1, heads))
    hm = np.kron(np.eye(heads, dtype=np.float32), np.ones((CHUNK, dk), np.float32))
    wd = jnp.zeros((2, LANES, qk), F32)
    wd = wd.at[0, 0:rank].set(w_dec[0]).at[1, rank:2 * rank].set(w_dec[1])
    full = lambda a: pl.BlockSpec(a.shape, lambda i, j: (0,) * a.ndim)
    consts = [jnp.asarray(cst), jnp.asarray(msk), jnp.asarray(hm), wd, b_dec.reshape(2, 1, qk)]
    return pl.pallas_call(
        functools.partial(_gla_kernel, heads=heads, dk=dk, dv=dv),
        grid=(b, nc),
        in_specs=[pl.BlockSpec((1, blk, cols), lambda i, j: (i, j, 0)),
                  pl.BlockSpec((1, blk, cols), lambda i, j: (i, nc - 1 - j, 0)),
                  pl.BlockSpec((1, 2, dv, qk), lambda i, j: (i, 0, 0, 0))] + [full(a) for a in consts],
        out_specs=[pl.BlockSpec((1, blk, heads * dv), lambda i, j: (i, j, 0)),
                   pl.BlockSpec((1, blk, heads * dv), lambda i, j: (i, nc - 1 - j, 0)),
                   pl.BlockSpec((1, 2, dv, qk), lambda i, j: (i, 0, 0, 0))],
        out_shape=[jax.ShapeDtypeStruct((b, t, heads * dv), F32),
                   jax.ShapeDtypeStruct((b, t, heads * dv), F32),
                   jax.ShapeDtypeStruct((b, 2, dv, qk), F32)],
        scratch_shapes=[pltpu.VMEM((2, dv, qk), F32)],
        compiler_params=_cparams(("parallel", "arbitrary")),
    )(u_gla, u_gla, s0c, *consts)


def _gla_state_in(s):
    h, dk, dv = s.shape[-3:]
    return jnp.moveaxis(s, -1, -3).reshape(s.shape[:-3] + (dv, h * dk))


def _gla_state_out(s, heads):
    dv, qk = s.shape[-2:]
    return jnp.moveaxis(s.reshape(s.shape[:-2] + (dv, heads, qk // heads)), -3, -1)


def _dot3(x, w_ref):
    hi, lo = _two_terms(x)
    return _dot(hi, w_ref[0]) + _dot(lo, w_ref[0]) + _dot(hi, w_ref[1])


def _rwkv_kernel(uf_ref, ub_ref, s0_ref, msk_ref, w2_ref, w0_ref, a2_ref, a0_ref, g2_ref, kk_ref, ka_ref, rk_ref,
                 bones_ref, yf_ref, yb_ref, bonus_ref, gate_ref, sf_ref, s_scr, *, heads, n):
    ci = pl.program_id(1)
    g2h = 2 * heads

    @pl.when(ci == 0)
    def _():
        s_scr[...] = s0_ref[0].reshape(g2h, n, n)

    c = CHUNK
    wd = heads * n
    half = bones_ref.shape[0]
    bones = bones_ref[...]
    dirs = (0, 1)
    chains = [(d, (o if d == 0 else RWKV_STEP_CHUNKS - 1 - o)) for o in range(RWKV_STEP_CHUNKS) for d in dirs]
    nch = range(len(chains))

    def seg_sum(x):
        return jnp.concatenate([_split_dot(x[:, i:i + half], bones) for i in range(0, wd, half)], axis=-1)

    def split_heads(x):
        return jnp.stack([x[:, h * n:(h + 1) * n] for h in range(heads)], axis=0)

    def masked(x, mi):
        return jnp.concatenate([jnp.where(msk_ref[chains[i][0], mi] > 0, x[i * heads:(i + 1) * heads], 0.0)
                                for i in nch], axis=0)

    us = [(uf_ref, ub_ref)[d][0, rb * c:(rb + 1) * c, :] for d, rb in chains]
    z0 = 3 * wd
    r = [u[:, 0:wd] for u in us]
    k = [u[:, wd:2 * wd] for u in us]
    v = [u[:, 2 * wd:3 * wd] for u in us]
    wlog = [-_softplus(-(w0_ref[chains[i][0]] + _dot3(jnp.tanh(us[i][:, z0 + chains[i][0] * n:z0 + (chains[i][0] + 1) * n]),
                                                      w2_ref.at[chains[i][0]]))) - 0.5 for i in nch]
    logw = [-jnp.exp(wlog[i]) for i in nch]
    a = [_sigmoid(a0_ref[...] + _dot3(us[i][:, z0 + 2 * n:z0 + 3 * n], a2_ref)) for i in nch]
    kk = [k[i] * kk_ref[...] for i in nch]
    kk = [kk[i] * lax.rsqrt(jnp.maximum(seg_sum(kk[i] * kk[i]), 1e-24)) for i in nch]
    km = [k[i] * (1.0 + (a[i] - 1.0) * ka_ref[...]) for i in nch]
    bb = [kk[i] * a[i] for i in nch]
    cum = [_split_dot_l(msk_ref[chains[i][0], 0], logw[i]) for i in nch]
    tot = [jnp.sum(logw[i], axis=0, keepdims=True) for i in nch]
    e_neg = [jnp.exp(-cum[i]) for i in nch]
    e_rest = [jnp.exp(tot[i] - cum[i]) for i in nch]
    e_tot = [jnp.exp(tot[i]) for i in nch]
    ar = jnp.concatenate([split_heads(jnp.concatenate([kk[i] * jnp.exp(cum[i] - logw[i]), r[i] * jnp.exp(cum[i])],
                                                      axis=0)) for i in nch], axis=0)
    kb = jnp.concatenate([split_heads(jnp.concatenate([km[i] * e_neg[i], bb[i] * e_neg[i]], axis=0)) for i in nch],
                         axis=0)
    vv = jnp.concatenate([split_heads(v[i]) for i in nch], axis=0)
    kbh = [jnp.concatenate([km[i] * e_rest[i], bb[i] * e_rest[i]], axis=0) for i in nch]
    for i in nch:
        d, rb = chains[i]
        if d == 0:
            bonus_ref[0, rb * c:(rb + 1) * c, :] = seg_sum(r[i] * km[i] * rk_ref[...]) * v[i]
            gate_ref[0, rb * c:(rb + 1) * c, :] = _dot3(
                _sigmoid(us[i][:, z0 + 3 * n:z0 + 3 * n + g2_ref.shape[1]]), g2_ref)

    sc = _bmm(ar, kb, BNT)
    lak = masked(sc[:, :c, :c], 1)
    lab = masked(sc[:, :c, c:], 1)
    mrk = masked(sc[:, c:, :c], 0)
    mrb = masked(sc[:, c:, c:], 0)
    lv = _bmm(lak, vv)
    ld = masked(lab, 2)
    x = msk_ref[0, 5] - ld
    p = _bmm(ld, ld)
    x = x + _bmm(x, p)
    p = _bmm(p, p)
    x = x + _bmm(x, p)
    p = _bmm(p, p)
    x = x + _bmm(x, p)
    for mi in (3, 4):
        x = x - _bmm(x, _bmm(masked(lab, mi), x))

    st = s_scr[...]
    for o in range(RWKV_STEP_CHUNKS):
        sl = slice(o * g2h, (o + 1) * g2h)
        through = _bmm(ar[sl], st, BNT)
        uu = _bmm(x[sl], through[:, :c] + lv[sl])
        vu = jnp.concatenate([vv[sl], -uu], axis=1)
        y = through[:, c:] + _bmm(jnp.concatenate([mrk[sl], mrb[sl]], axis=2), vu)
        new_st = []
        for d in dirs:
            i = o * 2 + d
            rb = chains[i][1]
            (yf_ref, yb_ref)[d][0, rb * c:(rb + 1) * c, :] = jnp.concatenate(
                [y[d * heads + h] for h in range(heads)], axis=-1)
            for h in range(heads):
                hs = slice(h * n, (h + 1) * n)
                g = d * heads + h
                new_st.append(st[g] * e_tot[i][:, hs] + _bdot(vu[g], kbh[i][:, hs], TN))
        st = jnp.stack(new_st, axis=0)
    s_scr[...] = st

    @pl.when(ci == pl.num_programs(1) - 1)
    def _():
        sf_ref[0] = s_scr[...].reshape(2, heads, n, n)


def rwkv(u, s0, w2, w0, a2, a0, g2, k_k, k_a, r_k, heads, n):
    b, t, cols = u.shape
    blk = RWKV_STEP_CHUNKS * CHUNK
    nc = t // blk
    wd = heads * n
    msk = _rwkv_consts()
    bones = np.kron(np.eye(heads // 2, dtype=np.float32), np.ones((n, n), np.float32))
    row = lambda x: x.reshape(1, wd)
    full = lambda a: pl.BlockSpec(a.shape, lambda i, j: (0,) * a.ndim)
    terms = lambda w: jnp.stack(_two_terms(w), axis=-3)
    args = [jnp.asarray(msk), terms(w2), w0.reshape(2, 1, wd), terms(a2), row(a0), terms(g2), row(k_k), row(k_a),
            row(r_k), jnp.asarray(bones)]
    return pl.pallas_call(
        functools.partial(_rwkv_kernel, heads=heads, n=n),
        grid=(b, nc),
        in_specs=[pl.BlockSpec((1, blk, cols), lambda i, j: (i, j, 0)),
                  pl.BlockSpec((1, blk, cols), lambda i, j: (i, nc - 1 - j, 0)),
                  pl.BlockSpec((1, 2, heads, n, n), lambda i, j: (i, 0, 0, 0, 0))] + [full(a) for a in args],
        out_specs=[pl.BlockSpec((1, blk, wd), lambda i, j: (i, j, 0)),
                   pl.BlockSpec((1, blk, wd), lambda i, j: (i, nc - 1 - j, 0)),
                   pl.BlockSpec((1, blk, wd), lambda i, j: (i, j, 0)),
                   pl.BlockSpec((1, blk, wd), lambda i, j: (i, j, 0)),
                   pl.BlockSpec((1, 2, heads, n, n), lambda i, j: (i, 0, 0, 0, 0))],
        out_shape=[jax.ShapeDtypeStruct((b, t, wd), F32)] * 4
        + [jax.ShapeDtypeStruct((b, 2, heads, n, n), F32)],
        scratch_shapes=[pltpu.VMEM((2 * heads, n, n), F32)],
        compiler_params=_cparams(("parallel", "arbitrary")),
    )(u, u, s0, *args)


def _proj_out_kernel(x_ref, mod_ref, of_ref, ob_ref, gg_ref, gn_ref, yf_ref, yb_ref, bonus_ref, gate_ref,
                     lng_ref, lnb_ref, bmean_ref, w_ref, o_ref, *, gla_heads):
    m = mod_ref[0]
    o = of_ref[0] + ob_ref[0]
    gw = o.shape[-1]
    dv = gw // gla_heads
    parts = []
    for h in range(gla_heads):
        oh = o[:, h * dv:(h + 1) * dv]
        parts.append(oh * lax.rsqrt(jnp.mean(oh * oh, axis=-1, keepdims=True) + EPS))
    o_gla = jnp.concatenate(parts, axis=-1) * gn_ref[...] * _silu(gg_ref[0])
    y = yf_ref[0] + yb_ref[0]
    bmean = bmean_ref[...]
    dy = y - _split_dot(y, bmean)
    var = _split_dot(dy * dy, bmean)
    yn = dy * lax.rsqrt(var + RWKV_LN_EPS) * lng_ref[...] + lnb_ref[...]
    o_rwkv = (yn + bonus_ref[0]) * gate_ref[0]
    w = w_ref[...]
    mix = _bdot(o_gla, w[:gw]) + _bdot(o_rwkv, w[gw:])
    o_ref[0] = x_ref[0] + m[2:3] * mix


def proj_out(x, mod, mod_row, o_f, o_b, u_gla, gla_norm, yf, yb, bonus, gate, ln_g, ln_b, w_out, gla_heads, rwkv_n,
             tm=256):
    b, t, d = x.shape
    gw = o_f.shape[-1]
    rw = yf.shape[-1]
    tm = min(tm, t)
    bmean = np.kron(np.eye(rw // rwkv_n, dtype=np.float32), np.full((rwkv_n, rwkv_n), 1.0 / rwkv_n, np.float32))
    tile = lambda wdt: pl.BlockSpec((1, tm, wdt), lambda i, j: (i, j, 0))
    row = lambda wdt: pl.BlockSpec((1, wdt), lambda i, j: (0, 0))
    g_col_block = (u_gla.shape[-1] - LANES - gw) // gw
    return pl.pallas_call(
        functools.partial(_proj_out_kernel, gla_heads=gla_heads),
        grid=(b, t // tm),
        in_specs=[tile(d),
                  pl.BlockSpec((1, 6, d), lambda i, j: (mod_row(i), 0, 0)),
                  tile(gw), tile(gw),
                  pl.BlockSpec((1, tm, gw), lambda i, j: (i, j, g_col_block)),
                  row(gw), tile(rw), tile(rw), tile(rw), tile(rw), row(rw), row(rw),
                  pl.BlockSpec((rw, rw), lambda i, j: (0, 0)),
                  pl.BlockSpec(w_out.shape, lambda i, j: (0, 0))],
        out_specs=tile(d),
        out_shape=jax.ShapeDtypeStruct((b, t, d), F32),
        compiler_params=_cparams(("parallel", "parallel")),
    )(x, mod, o_f, o_b, u_gla, gla_norm.reshape(1, gw), yf, yb, bonus, gate, ln_g.reshape(1, rw),
      ln_b.reshape(1, rw), jnp.asarray(bmean), w_out)


ROUTE_TILE = 256
MOE_SUB = 128
MOE_CAP = 32
MOE_SUPER = 1024


def _route_kernel(x_ref, mod_ref, g_ref, rw_ref, rb_ref, tri_ref, h_ref, gates_ref, pos_ref, *, n_exp):
    m = mod_ref[0]
    h = _rms_mod(x_ref[...], g_ref[...], m[4:5], m[3:4])
    h_ref[...] = h.astype(BF16)
    tm = h.shape[0]
    gsz = n_exp // N_GROUPS
    scores = _sigmoid(_dot(rw_ref[...], h, NT, precision=HI))
    sel = scores + rb_ref[...]
    neg = jnp.float32(-jnp.inf)
    s3 = sel.reshape(N_GROUPS, gsz, tm)
    idx = lax.broadcasted_iota(jnp.int32, s3.shape, 1)
    m1 = jnp.max(s3, axis=1, keepdims=True)
    first = jnp.min(jnp.where(s3 == m1, idx, gsz), axis=1, keepdims=True)
    m2 = jnp.max(jnp.where(idx == first, neg, s3), axis=1, keepdims=True)
    gs = (m1 + m2).reshape(N_GROUPS, tm)
    gidx = lax.broadcasted_iota(jnp.int32, gs.shape, 0)
    cnt = jnp.zeros(gs.shape, jnp.int32)
    for g in range(N_GROUPS):
        other = gs[g:g + 1]
        cnt = cnt + ((other > gs) | ((other == gs) & (g < gidx))).astype(jnp.int32)
    gkeep = jnp.broadcast_to((cnt < TOPK_GROUPS)[:, None, :], s3.shape).reshape(n_exp, tm)
    selm = jnp.where(gkeep, sel, neg)
    eidx = lax.broadcasted_iota(jnp.int32, selm.shape, 0)
    rank = jnp.zeros(selm.shape, jnp.int32)
    for e in range(n_exp):
        other = selm[e:e + 1]
        rank = rank + ((other > selm) | ((other == selm) & (e < eidx))).astype(jnp.int32)
    chosen = rank < TOP_K
    wts = jnp.where(chosen, scores, 0.0)
    gates_ref[...] = wts / jnp.sum(wts, axis=0, keepdims=True) * ROUTED_SCALE
    pos_ref[...] = _dot(chosen.astype(BF16), tri_ref[...])


def route(x2, mod, mod_row, norm_g, router_w, router_b, tm, sub):
    nt, d = x2.shape
    n_exp = router_w.shape[1]
    tri = np.triu(np.ones((tm, tm), np.float32), 1) * np.kron(np.eye(tm // sub), np.ones((sub, sub)))
    return pl.pallas_call(
        functools.partial(_route_kernel, n_exp=n_exp),
        grid=(nt // tm,),
        in_specs=[pl.BlockSpec((tm, d), lambda i: (i, 0)),
                  pl.BlockSpec((1, 6, d), lambda i: (mod_row(i, tm), 0, 0)),
                  pl.BlockSpec((1, d), lambda i: (0, 0)),
                  pl.BlockSpec((n_exp, d), lambda i: (0, 0)),
                  pl.BlockSpec((n_exp, 1), lambda i: (0, 0)),
                  pl.BlockSpec((tm, tm), lambda i: (0, 0))],
        out_specs=[pl.BlockSpec((tm, d), lambda i: (i, 0)),
                   pl.BlockSpec((n_exp, tm), lambda i: (0, i)),
                   pl.BlockSpec((n_exp, tm), lambda i: (0, i))],
        out_shape=[jax.ShapeDtypeStruct((nt, d), BF16),
                   jax.ShapeDtypeStruct((n_exp, nt), F32),
                   jax.ShapeDtypeStruct((n_exp, nt), F32)],
        compiler_params=_cparams(("parallel",)),
    )(x2, mod, norm_g.reshape(1, d), router_w.T, router_b.reshape(n_exp, 1), jnp.asarray(tri, BF16))


def _swiglu(h, w1, w3, w2):
    return _bdot(_silu(_bdot(h, w1)) * _bdot(h, w3), w2)


def _moe_kernel(npass_ref, x_ref, mod_ref, h_ref, gates_ref, pos_ref, w1_ref, w3_ref, w2_ref, s1_ref, s3_ref, s2_ref,
                o_ref, xe_ref, ye_ref, *, sub, cap, group):
    i = pl.program_id(0)
    j = pl.program_id(1)
    n_grp = pl.num_programs(1)
    n_sub = h_ref.shape[0] // sub

    @pl.when(j == 0)
    def _():
        o_ref[...] = _swiglu(h_ref[...], s1_ref[...], s3_ref[...], s2_ref[...])

    first = (i * n_grp + j) * group
    gate = [gates_ref[pl.ds(j * group + x, 1), :] for x in range(group)]
    pos = [pos_ref[pl.ds(j * group + x, 1), :] for x in range(group)]
    need = [npass_ref[first + x] for x in range(group)]
    rows = lax.broadcasted_iota(jnp.int32, (cap, sub), 0).astype(F32)

    def one_pass(it, skip_done):
        base = it * float(cap) if isinstance(it, int) else (it * cap).astype(F32)
        picks = []
        for x in range(group):
            picks.append([])
            for s in range(n_sub):
                sl = slice(s * sub, (s + 1) * sub)
                g = gate[x][:, sl]
                pick = jnp.where((pos[x][:, sl] - base == rows) & (g > 0.0), g, 0.0)
                picks[x].append(pick.astype(BF16))

            def run_expert(x=x):
                for s in range(n_sub):
                    onehot = (picks[x][s] > 0.0).astype(BF16)
                    xe_ref[x, s * cap:(s + 1) * cap, :] = _dot(onehot, h_ref[s * sub:(s + 1) * sub, :]).astype(BF16)
                ye_ref[x] = _swiglu(xe_ref[x], w1_ref[x], w3_ref[x], w2_ref[x]).astype(BF16)

            if skip_done:
                pl.when(it < need[x])(run_expert)
        if not skip_done:
            for s in range(n_sub):
                onehot = jnp.concatenate([(picks[x][s] > 0.0).astype(BF16) for x in range(group)], axis=0)
                rows_all = _dot(onehot, h_ref[s * sub:(s + 1) * sub, :]).astype(BF16)
                for x in range(group):
                    xe_ref[x, s * cap:(s + 1) * cap, :] = rows_all[x * cap:(x + 1) * cap]
            mid = [_silu(_bdot(xe_ref[x], w1_ref[x])) * _bdot(xe_ref[x], w3_ref[x]) for x in range(group)]
            for x in range(group):
                ye_ref[x] = _bdot(mid[x], w2_ref[x]).astype(BF16)
        for s in range(n_sub):
            sl = slice(s * sub, (s + 1) * sub)
            pick_all = jnp.concatenate([picks[x][s] for x in range(group)], axis=0)
            ye_all = jnp.concatenate([ye_ref[x, s * cap:(s + 1) * cap, :] for x in range(group)], axis=0)
            o_ref[sl, :] += _dot(pick_all, ye_all, TN)

    one_pass(0, False)
    n_pass = need[0]
    for x in range(1, group):
        n_pass = jnp.maximum(n_pass, need[x])

    def later_pass(it, carry):
        one_pass(it, True)
        return carry

    lax.fori_loop(1, n_pass, later_pass, 0)

    @pl.when(j == n_grp - 1)
    def _():
        o_ref[...] = x_ref[...] + mod_ref[0][5:6] * o_ref[...]


def moe(x2, mod, mod_row, h, gates, pos, w1, w3, w2, s1, s3, s2, sub, tm):
    nt, d = x2.shape
    n_exp, _, ff = w1.shape
    cap = min(MOE_CAP, sub)
    group = LANES // cap
    n_sub = tm // sub
    cnt = (pos + (gates > 0.0))[:, sub - 1::sub].reshape(n_exp, nt // tm, n_sub).max(axis=-1)
    npass = ((cnt.astype(jnp.int32) + cap - 1) // cap).T.reshape(-1)
    full = lambda a: pl.BlockSpec(a.shape, lambda i, j, n: (0,) * a.ndim)
    grid_spec = pltpu.PrefetchScalarGridSpec(
        num_scalar_prefetch=1,
        grid=(nt // tm, n_exp // group),
        in_specs=[pl.BlockSpec((tm, d), lambda i, j, n: (i, 0)),
                  pl.BlockSpec((1, 6, d), lambda i, j, n: (mod_row(i, tm), 0, 0)),
                  pl.BlockSpec((tm, d), lambda i, j, n: (i, 0)),
                  pl.BlockSpec((n_exp, tm), lambda i, j, n: (0, i)),
                  pl.BlockSpec((n_exp, tm), lambda i, j, n: (0, i)),
                  pl.BlockSpec((group, d, ff), lambda i, j, n: (j, 0, 0)),
                  pl.BlockSpec((group, d, ff), lambda i, j, n: (j, 0, 0)),
                  pl.BlockSpec((group, ff, d), lambda i, j, n: (j, 0, 0)),
                  full(s1), full(s3), full(s2)],
        out_specs=pl.BlockSpec((tm, d), lambda i, j, n: (i, 0)),
        scratch_shapes=[pltpu.VMEM((group, n_sub * cap, d), BF16), pltpu.VMEM((group, n_sub * cap, d), BF16)])
    return pl.pallas_call(
        functools.partial(_moe_kernel, sub=sub, cap=cap, group=group),
        grid_spec=grid_spec,
        out_shape=jax.ShapeDtypeStruct((nt, d), F32),
        compiler_params=_cparams(("parallel", "arbitrary")),
    )(npass, x2, mod, h, gates, pos, w1, w3, w2, s1, s3, s2)


def _final_kernel(x_ref, g_ref, o_ref):
    x = x_ref[...]
    o_ref[...] = x * lax.rsqrt(jnp.mean(x * x, axis=-1, keepdims=True) + EPS) * g_ref[...]


def final_norm(x2, g, tm=512):
    nt, d = x2.shape
    tm = min(tm, nt)
    return pl.pallas_call(
        _final_kernel,
        grid=(nt // tm,),
        in_specs=[pl.BlockSpec((tm, d), lambda i: (i, 0)), pl.BlockSpec((1, d), lambda i: (0, 0))],
        out_specs=pl.BlockSpec((tm, d), lambda i: (i, 0)),
        out_shape=jax.ShapeDtypeStruct((nt, d), F32),
        compiler_params=_cparams(("parallel",)),
    )(x2, g.reshape(1, d))


def _pad_cols(a, n):
    return jnp.pad(a, [(0, 0)] * (a.ndim - 1) + [(0, n - a.shape[-1])])


def _round_up(n, m):
    return -(-n // m) * m


def _layer(x, mod, latent, s_gla_c, s_rwkv, p, dims, grid_w):
    b, t, d = x.shape
    ctx_row = mod.shape[0] - 1
    brow = (lambda i: i) if latent else (lambda i: ctx_row)
    trow = (lambda i, tm: (i * tm) // t) if latent else (lambda i, tm: ctx_row)
    u_gla, u_rwkv = proj_in(x, mod, brow, p['norm1'], p['w_in'], dims['gla_cols'])
    hh, ww = (t // grid_w, grid_w) if latent else (b, t)
    u_rwkv = conv(u_rwkv, p['rwkv_conv'], hh, ww, vertical=latent)
    o_f, o_b, sg = gla(u_gla, s_gla_c, p['gla_w_dec'], p['gla_b_dec'], dims['gla_heads'], dims['gla_dk'], dims['gla_dv'])
    yf, yb, bonus, gate, sr = rwkv(u_rwkv, s_rwkv, p['rwkv_w2'], p['rwkv_w0'], p['rwkv_a2'], p['rwkv_a0'],
                                   p['rwkv_g2'], p['rwkv_k_k'], p['rwkv_k_a'], p['rwkv_r_k'],
                                   dims['rwkv_heads'], dims['rwkv_n'])
    x = proj_out(x, mod, brow, o_f, o_b, u_gla, p['gla_norm'], yf, yb, bonus, gate, p['rwkv_ln_g'], p['rwkv_ln_b'],
                 p['w_out'], dims['gla_heads'], dims['rwkv_n'])
    x2 = x.reshape(b * t, d)
    sub = min(MOE_SUB, t if latent else b * t)
    tm = min(MOE_SUPER, t if latent else b * t)
    h, gates, pos = route(x2, mod, trow, p['norm2'], p['router_w'], p['router_b'], min(ROUTE_TILE, tm), sub)
    x2 = moe(x2, mod, trow, h, gates, pos, p['exp_w1'], p['exp_w3'], p['exp_w2'], p['sh_w1'], p['sh_w3'], p['sh_w2'],
             sub, tm)
    return x2.reshape(b, t, d), sg, sr


def kernel(x_prompt, x_sample, c, state_gla, state_rwkv, c_ctx, ada_w, ada_b, norm1, norm2, norm_f, w_in, w_out,
           gla_w_dec, gla_b_dec, gla_norm, rwkv_conv, rwkv_w2, rwkv_w0, rwkv_a2, rwkv_a0, rwkv_g2, rwkv_k_k, rwkv_k_a,
           rwkv_r_k, rwkv_ln_g, rwkv_ln_b, router_w, router_b, exp_w1, exp_w3, exp_w2, sh_w1, sh_w3, sh_w2):
    depth, d, _ = ada_w.shape
    bp = x_prompt.shape[0]
    bs = x_sample.shape[0]
    grid_w = 64
    gla_heads, gla_dk, gla_dv = state_gla.shape[3:]
    rwkv_heads, rwkv_n = state_rwkv.shape[3:5]
    gla_qk, gla_w, rwkv_w = gla_heads * gla_dk, gla_heads * gla_dv, rwkv_heads * rwkv_n
    gla_cols_raw = 2 * gla_qk + 2 * gla_w + 2 * gla_w_dec.shape[2]
    rwkv_cols_raw = w_in.shape[2] - gla_cols_raw
    gla_cols = 2 * gla_qk + 2 * gla_w + LANES
    rwkv_cols = _round_up(rwkv_cols_raw, LANES)
    dims = dict(gla_cols=gla_cols, gla_heads=gla_heads, gla_dk=gla_dk, gla_dv=gla_dv,
                rwkv_heads=rwkv_heads, rwkv_n=rwkv_n)

    w_in_p = jnp.concatenate([_pad_cols(w_in[..., :gla_cols_raw], gla_cols),
                              _pad_cols(w_in[..., gla_cols_raw:], rwkv_cols)], axis=-1).astype(BF16)
    conv_p = _pad_cols(rwkv_conv, rwkv_cols)
    w_out_b = w_out.astype(BF16)
    e1, e3, e2 = exp_w1.astype(BF16), exp_w3.astype(BF16), exp_w2.astype(BF16)
    s1, s3, s2 = sh_w1.astype(BF16), sh_w3.astype(BF16), sh_w2.astype(BF16)

    rows = _round_up(bs + 1, 8)
    cvec = jnp.concatenate([c, c_ctx[None], jnp.zeros((rows - bs - 1, d), F32)], axis=0)
    mod_all = adaln(cvec, ada_w, ada_b).reshape(depth, rows, 6, d)[:, :bs + 1]

    zero_gla = jnp.zeros((bp, 2, gla_dv, gla_qk), F32)
    zero_rwkv = jnp.zeros((bp, 2, rwkv_heads, rwkv_n, rwkv_n), F32)
    state_gla_c = _gla_state_in(state_gla)

    xp, xs = x_prompt, x_sample
    new_gla, new_rwkv = [], []
    for l in range(depth):
        p = dict(norm1=norm1[l], norm2=norm2[l], w_in=w_in_p[l], w_out=w_out_b[l], gla_w_dec=gla_w_dec[l],
                 gla_b_dec=gla_b_dec[l], gla_norm=gla_norm[l], rwkv_conv=conv_p[l], rwkv_w2=rwkv_w2[l],
                 rwkv_w0=rwkv_w0[l], rwkv_a2=rwkv_a2[l], rwkv_a0=rwkv_a0[l], rwkv_g2=rwkv_g2[l],
                 rwkv_k_k=rwkv_k_k[l], rwkv_k_a=rwkv_k_a[l], rwkv_r_k=rwkv_r_k[l], rwkv_ln_g=rwkv_ln_g[l],
                 rwkv_ln_b=rwkv_ln_b[l], router_w=router_w[l], router_b=router_b[l], exp_w1=e1[l], exp_w3=e3[l],
                 exp_w2=e2[l], sh_w1=s1[l], sh_w3=s3[l], sh_w2=s2[l])
        mod = mod_all[l]
        xp, sg, sr = _layer(xp, mod, False, zero_gla, zero_rwkv, p, dims, grid_w)
        new_gla.append(_gla_state_out(sg, gla_heads))
        new_rwkv.append(sr)
        xs, _, _ = _layer(xs, mod, True, state_gla_c[:, l], state_rwkv[:, l], p, dims, grid_w)
    y_prompt = final_norm(xp.reshape(-1, d), norm_f).reshape(xp.shape)
    y_sample = final_norm(xs.reshape(-1, d), norm_f).reshape(xs.shape)
    return (y_prompt, y_sample, jnp.stack(new_gla, axis=1), jnp.stack(new_rwkv, axis=1))
```
